```python
import math
import jax, jax.numpy as jnp
from jax import lax
import numpy as np


D_MODEL = 2048
BATCH = 1
SEQ = 8192
DEPTH = 1
DEC_BATCH = 32
DEC_SEQ = 8
PAST_LEN = 16384
PAGE_SIZE = 128

MIX_WIDTH = D_MODEL
DIFF_HEADS = 4
DIFF_DK = 128
DIFF_DV = 2 * DIFF_DK
GLA_HEADS = 4
GLA_DK = 128
GLA_DV = 256
GLA_RANK = 16
GLA_TAU = 16.0
GLA_CHUNK = 64
ROPE_THETA = 500000.0
ROT_DIM = DIFF_DK // 4
D_FF = 5632
CONV_W = 3
EPS = 1e-6
Q_BLOCK = 128

DIFF_QK = DIFF_HEADS * 2 * DIFF_DK
DIFF_V = DIFF_HEADS * DIFF_DV
GLA_QK = GLA_HEADS * GLA_DK
GLA_V = GLA_HEADS * GLA_DV
IN_SIZES = (DIFF_QK, DIFF_QK, DIFF_V, GLA_QK, GLA_QK, GLA_V, GLA_V, GLA_RANK)
N_IN = DIFF_QK * 2 + DIFF_V + GLA_QK * 2 + GLA_V * 2 + GLA_RANK

kernel_name = "hymba_diffattn_gla_convffn_step"


def rmsnorm(x, g):
    xf = x.astype(jnp.float32)
    y = xf * lax.rsqrt(jnp.mean(xf * xf, axis=-1, keepdims=True) + EPS)
    return (y * g.astype(jnp.float32)).astype(x.dtype)


def partial_rope(x, pos):
    half = ROT_DIM // 2
    inv_freq = ROPE_THETA ** (-jnp.arange(half, dtype=jnp.float32) * (2.0 / ROT_DIM))
    ang = pos.astype(jnp.float32)[:, None] * inv_freq[None, :]
    cos = jnp.cos(ang)[None, :, None, None, :]
    sin = jnp.sin(ang)[None, :, None, None, :]
    xf = x.astype(jnp.float32)
    x1 = xf[..., :half]
    x2 = xf[..., half:ROT_DIM]
    out = jnp.concatenate([x1 * cos - x2 * sin, x2 * cos + x1 * sin, xf[..., ROT_DIM:]], axis=-1)
    return out.astype(x.dtype)


def mixer_projections(xn, pos, w_in, w_g2, b_g):
    B, T, _ = xn.shape
    idx = []
    acc = 0
    for s in IN_SIZES[:-1]:
        acc += s
        idx.append(acc)
    dq, dk, dv, gq, gk, gv, gr, glr = jnp.split(xn @ w_in, idx, axis=-1)
    dq = partial_rope(dq.reshape(B, T, DIFF_HEADS, 2, DIFF_DK), pos) * (DIFF_DK ** -0.5)
    dk = partial_rope(dk.reshape(B, T, DIFF_HEADS, 2, DIFF_DK), pos)
    dv = dv.reshape(B, T, DIFF_HEADS, DIFF_DV)
    gq = gq.reshape(B, T, GLA_HEADS, GLA_DK) * (GLA_DK ** -0.5)
    gk = gk.reshape(B, T, GLA_HEADS, GLA_DK)
    gv = gv.reshape(B, T, GLA_HEADS, GLA_DV)
    log_a = jax.nn.log_sigmoid((glr @ w_g2 + b_g).astype(jnp.float32)) / GLA_TAU
    log_a = log_a.reshape(B, T, GLA_HEADS, GLA_DK)
    return dq, dk, dv, gq, gk, gv, log_a, gr


def diff_combine(s, lam):
    p = jax.nn.softmax(s, axis=-1)
    return p[:, :, 0] - lam * p[:, :, 1]


def diff_attn_prompt(q, k, v, lam):
    B, T, H, _, _ = q.shape
    nb = T // Q_BLOCK
    qb = jnp.moveaxis(q.reshape(B, nb, Q_BLOCK, H, 2, DIFF_DK), 1, 0)
    kpos = jnp.arange(T)

    def one_block(args):
        qblk, start = args
        s = jnp.einsum('bqhmd,bkhmd->bhmqk', qblk, k).astype(jnp.float32)
        qpos = start + jnp.arange(Q_BLOCK)
        mask = kpos[None, :] <= qpos[:, None]
        a = diff_combine(jnp.where(mask, s, -jnp.inf), lam)
        return jnp.einsum('bhqk,bkhv->bqhv', a.astype(v.dtype), v)

    o = lax.map(one_block, (qb, jnp.arange(nb) * Q_BLOCK))
    return jnp.moveaxis(o, 0, 1).reshape(B, T, H, DIFF_DV)


def diff_attn_sample(q, k_new, v_new, k_past, v_past, lam):
    T = q.shape[1]
    P = k_past.shape[1]
    s_past = jnp.einsum('bqhmd,bkhmd->bhmqk', q, k_past).astype(jnp.float32)
    s_new = jnp.einsum('bqhmd,bkhmd->bhmqk', q, k_new).astype(jnp.float32)
    causal = jnp.tril(jnp.ones((T, T), dtype=bool))
    s_new = jnp.where(causal, s_new, -jnp.inf)
    a = diff_combine(jnp.concatenate([s_past, s_new], axis=-1), lam).astype(v_new.dtype)
    return (jnp.einsum('bhqk,bkhv->bqhv', a[..., :P], v_past)
            + jnp.einsum('bhqk,bkhv->bqhv', a[..., P:], v_new))


def gla_recurrence(q, k, v, log_a, S0):
    B, T, H, _ = q.shape
    C = min(GLA_CHUNK, T)
    Tp = -(-T // C) * C
    pad = ((0, 0), (0, Tp - T), (0, 0), (0, 0))
    nc = Tp // C

    def to_chunks(a):
        a = jnp.pad(a.astype(jnp.float32), pad)
        return a.reshape(B, nc, C, H, a.shape[-1]).transpose(1, 0, 3, 2, 4)

    qc, kc, vc, gc = to_chunks(q), to_chunks(k), to_chunks(v), to_chunks(log_a)
    mask = jnp.tril(jnp.ones((C, C), dtype=bool))[..., None]

    def step(S, inp):
        qi, ki, vi, gi = inp
        bcum = jnp.cumsum(gi, axis=2)
        o_inter = jnp.einsum('bhtd,bhdv->bhtv', qi * jnp.exp(bcum), S)
        diff = bcum[:, :, :, None, :] - bcum[:, :, None, :, :]
        decay = jnp.exp(jnp.where(mask, diff, -jnp.inf))
        att = jnp.einsum('bhtd,bhsd,bhtsd->bhts', qi, ki, decay)
        o_intra = jnp.einsum('bhts,bhsv->bhtv', att, vi)
        blast = bcum[:, :, -1:, :]
        S_new = (jnp.exp(blast[:, :, 0, :])[..., None] * S
                 + jnp.einsum('bhsd,bhsv->bhdv', ki * jnp.exp(blast - bcum), vi))
        return S_new, o_inter + o_intra

    S_fin, o = lax.scan(step, S0.astype(jnp.float32), (qc, kc, vc, gc))
    o = o.transpose(1, 0, 3, 2, 4).reshape(B, Tp, H, GLA_DV)[:, :T]
    return o, S_fin


def conv_ffn(h, buf, g_ffn, w_up, w_conv, b_conv, w_down):
    T = h.shape[1]
    gate, val = jnp.split(rmsnorm(h, g_ffn) @ w_up, 2, axis=-1)
    full = jnp.concatenate([buf.astype(gate.dtype), gate], axis=1)
    conv = b_conv
    for j in range(CONV_W):
        conv = conv + w_conv[j] * full[:, j:j + T]
    out = (jax.nn.silu(conv) * val) @ w_down
    return out, full[:, T:]


def trunk_layer(x, pos, attend, S0, conv_buf, lam_init, g_mix, w_in, lam_q1, lam_k1, lam_q2, lam_k2,
                g_subln, w_g2, b_g, g_gla, w_out, g_ffn, w_up, w_conv, b_conv, w_down):
    B, T, _ = x.shape
    dq, dk, dv, gq, gk, gv, log_a, gr = mixer_projections(rmsnorm(x, g_mix), pos, w_in, w_g2, b_g)
    lam = (jnp.exp(jnp.sum(lam_q1.astype(jnp.float32) * lam_k1.astype(jnp.float32)))
           - jnp.exp(jnp.sum(lam_q2.astype(jnp.float32) * lam_k2.astype(jnp.float32))) + lam_init)
    o_diff = attend(dq, dk, dv, lam)
    o_diff = (rmsnorm(o_diff, g_subln) * (1.0 - lam_init)).reshape(B, T, DIFF_V)
    o_gla, S_fin = gla_recurrence(gq, gk, gv, log_a, S0)
    o_gla = rmsnorm(o_gla.astype(x.dtype), g_gla).reshape(B, T, GLA_V) * jax.nn.silu(gr)
    h = x + jnp.concatenate([o_diff, o_gla], axis=-1) @ w_out
    f, new_buf = conv_ffn(h, conv_buf, g_ffn, w_up, w_conv, b_conv, w_down)
    return h + f, dk, dv, S_fin, new_buf


def setup_inputs(seed: int = 0) -> dict:
    key = jax.random.key(seed)
    ks = jax.random.split(key, 24)
    f32 = jnp.float32
    n_pages = PAST_LEN // PAGE_SIZE
    used = DEC_BATCH * n_pages
    n_phys = used + max(1, used // 4)
    nrm = lambda k, shape, s: jax.random.normal(k, shape, f32) * s
    page_table = jax.random.permutation(ks[6], n_phys)[:used].reshape(DEC_BATCH, n_pages).astype(jnp.int32)
    return {
        "x_prompt": nrm(ks[0], (BATCH, SEQ, D_MODEL), 1.0),
        "x_sample": nrm(ks[1], (DEC_BATCH, DEC_SEQ, D_MODEL), 1.0),
        "cache_k": nrm(ks[2], (DEPTH, n_phys, PAGE_SIZE, DIFF_HEADS, 2, DIFF_DK), 1.0),
        "cache_v": nrm(ks[3], (DEPTH, n_phys, PAGE_SIZE, DIFF_HEADS, DIFF_DV), 1.0),
        "state_gla": nrm(ks[4], (DEPTH, DEC_BATCH, GLA_HEADS, GLA_DK, GLA_DV), 0.5),
        "state_ffn_conv": nrm(ks[5], (DEPTH, DEC_BATCH, CONV_W - 1, D_FF), 1.0),
        "page_table": page_table,
        "g_mix": 1.0 + nrm(ks[7], (DEPTH, D_MODEL), 0.02),
        "w_in": nrm(ks[8], (DEPTH, D_MODEL, N_IN), D_MODEL ** -0.5),
        "lam_q1": nrm(ks[9], (DEPTH, DIFF_DK), 0.1),
        "lam_k1": nrm(ks[10], (DEPTH, DIFF_DK), 0.1),
        "lam_q2": nrm(ks[11], (DEPTH, DIFF_DK), 0.1),
        "lam_k2": nrm(ks[12], (DEPTH, DIFF_DK), 0.1),
        "g_subln": 1.0 + nrm(ks[13], (DEPTH, DIFF_DV), 0.02),
        "w_g2": nrm(ks[14], (DEPTH, GLA_RANK, GLA_QK), GLA_RANK ** -0.5),
        "b_g": nrm(ks[15], (DEPTH, GLA_QK), 0.1),
        "g_gla": 1.0 + nrm(ks[16], (DEPTH, GLA_DV), 0.02),
        "w_out": nrm(ks[17], (DEPTH, MIX_WIDTH, D_MODEL), MIX_WIDTH ** -0.5),
        "g_ffn": 1.0 + nrm(ks[18], (DEPTH, D_MODEL), 0.02),
        "w_up": nrm(ks[19], (DEPTH, D_MODEL, 2 * D_FF), D_MODEL ** -0.5),
        "w_conv": nrm(ks[20], (DEPTH, CONV_W, D_FF), CONV_W ** -0.5),
        "b_conv": nrm(ks[21], (DEPTH, D_FF), 0.02),
        "w_down": nrm(ks[22], (DEPTH, D_FF, D_MODEL), D_FF ** -0.5),
        "g_final": 1.0 + nrm(ks[23], (D_MODEL,), 0.02),
    }


def reference(x_prompt, x_sample, cache_k, cache_v, state_gla, state_ffn_conv, page_table,
              g_mix, w_in, lam_q1, lam_k1, lam_q2, lam_k2, g_subln, w_g2, b_g, g_gla, w_out,
              g_ffn, w_up, w_conv, b_conv, w_down, g_final):
    B, T, _ = x_prompt.shape
    DB, TD, _ = x_sample.shape
    n_pages = page_table.shape[1]
    past_len = n_pages * cache_k.shape[2]
    pos_p = jnp.arange(T, dtype=jnp.int32)
    pos_s = past_len + jnp.arange(TD, dtype=jnp.int32)
    hp, hs = x_prompt, x_sample
    kp_l, vp_l, sp_l, bp_l = [], [], [], []
    ks_l, vs_l, ss_l, bs_l = [], [], [], []
    for l in range(DEPTH):
        lam_init = 0.8 - 0.6 * math.exp(-0.3 * l)
        wl = (g_mix[l], w_in[l], lam_q1[l], lam_k1[l], lam_q2[l], lam_k2[l], g_subln[l], w_g2[l], b_g[l],
              g_gla[l], w_out[l], g_ffn[l], w_up[l], w_conv[l], b_conv[l], w_down[l])
        k_past = cache_k[l][page_table].reshape(DB, past_len, DIFF_HEADS, 2, DIFF_DK)
        v_past = cache_v[l][page_table].reshape(DB, past_len, DIFF_HEADS, DIFF_DV)
        attend_s = lambda q, k, v, lam, kp=k_past, vp=v_past: diff_attn_sample(q, k, v, kp, vp, lam)
        S0p = jnp.zeros((B, GLA_HEADS, GLA_DK, GLA_DV), jnp.float32)
        buf0 = jnp.zeros((B, CONV_W - 1, D_FF), x_prompt.dtype)
        hp, kp, vp, Sp, bp = trunk_layer(hp, pos_p, diff_attn_prompt, S0p, buf0, lam_init, *wl)
        hs, ksn, vsn, Ss, bs = trunk_layer(hs, pos_s, attend_s, state_gla[l], state_ffn_conv[l], lam_init, *wl)
        kp_l.append(kp)
        vp_l.append(vp)
        sp_l.append(Sp.astype(state_gla.dtype))
        bp_l.append(bp.astype(state_ffn_conv.dtype))
        ks_l.append(ksn)
        vs_l.append(vsn)
        ss_l.append(Ss.astype(state_gla.dtype))
        bs_l.append(bs.astype(state_ffn_conv.dtype))
    y_prompt = rmsnorm(hp, g_final)
    y_sample = rmsnorm(hs, g_final)
    return (y_prompt, y_sample,
            jnp.stack(kp_l), jnp.stack(vp_l), jnp.stack(sp_l), jnp.stack(bp_l),
            jnp.stack(ks_l), jnp.stack(vs_l), jnp.stack(ss_l), jnp.stack(bs_l))
```

```python
import functools
import math

import jax
import jax.numpy as jnp
from jax import lax
from jax.experimental import pallas as pl
from jax.experimental.pallas import tpu as pltpu

F32 = jnp.float32
BF16 = jnp.bfloat16

D_MODEL = 2048
DIFF_HEADS = 4
DIFF_DK = 128
DIFF_DV = 256
GLA_HEADS = 4
GLA_DK = 128
GLA_DV = 256
GLA_RANK = 16
GLA_TAU = 16.0
GLA_CHUNK = 64
ROPE_THETA = 500000.0
ROT_DIM = DIFF_DK // 4
ROT_HALF = ROT_DIM // 2
D_FF = 5632
CONV_W = 3
EPS = 1e-6
LAM_INIT = 0.8 - 0.6 * math.exp(-0.3 * 0)

DIFF_QK = DIFF_HEADS * 2 * DIFF_DK
DIFF_V = DIFF_HEADS * DIFF_DV
GLA_QK = GLA_HEADS * GLA_DK
GLA_V = GLA_HEADS * GLA_DV
LANES = 128
SUBLANES = 8

VMEM_LIMIT = 56 * 1024 * 1024

PAGES_PER_STEP = 8
NEG_INF = float("-inf")


def _cparams(sem):
    return pltpu.CompilerParams(dimension_semantics=sem, vmem_limit_bytes=VMEM_LIMIT)


def _tile(n, pref):
    t = min(n, pref)
    assert n % t == 0, (n, pref)
    return t


def _rms(x, g):
    return x * lax.rsqrt(jnp.mean(x * x, axis=-1, keepdims=True) + EPS) * g


def _silu(x):
    return x * (1.0 / (1.0 + jnp.exp(-x)))


def _dot(a, b):
    return jnp.dot(a, b, preferred_element_type=F32)


def _dot_nt(a, b):
    return lax.dot_general(a, b, (((1,), (1,)), ((), ())), preferred_element_type=F32)


def _dot_tn(a, b):
    return lax.dot_general(a, b, (((0,), (0,)), ((), ())), preferred_element_type=F32)


def _norm_kernel(x_ref, g_ref, pos_ref, freq_ref, xn_ref, cos_ref, sin_ref):
    xn_ref[...] = _rms(x_ref[...], g_ref[...]).astype(xn_ref.dtype)
    ang = pos_ref[...] * freq_ref[...]
    lane = lax.broadcasted_iota(jnp.int32, ang.shape, 1)
    s = jnp.sin(ang)
    cos_ref[...] = jnp.cos(ang)
    sin_ref[...] = jnp.where(lane < ROT_HALF, -s, s)


def _norm(x, g, pos, freq_lane):
    n = x.shape[0]
    tm = _tile(n, 512)
    return pl.pallas_call(
        _norm_kernel,
        grid=(n // tm,),
        in_specs=[pl.BlockSpec((tm, D_MODEL), lambda i: (i, 0)),
                  pl.BlockSpec((1, D_MODEL), lambda i: (0, 0)),
                  pl.BlockSpec((tm, 1), lambda i: (i, 0)),
                  pl.BlockSpec((1, LANES), lambda i: (0, 0))],
        out_specs=[pl.BlockSpec((tm, D_MODEL), lambda i: (i, 0)),
                   pl.BlockSpec((tm, LANES), lambda i: (i, 0)),
                   pl.BlockSpec((tm, LANES), lambda i: (i, 0))],
        out_shape=[jax.ShapeDtypeStruct((n, D_MODEL), BF16),
                   jax.ShapeDtypeStruct((n, LANES), F32),
                   jax.ShapeDtypeStruct((n, LANES), F32)],
        compiler_params=_cparams(("parallel",)),
        name="norm",
    )(x, g, pos, freq_lane)


def _rope(y, cos, sin):
    lane = lax.broadcasted_iota(jnp.int32, cos.shape, 1)
    outs = []
    for s in range(y.shape[1] // LANES):
        x = y[:, s * LANES:(s + 1) * LANES]
        partner = jnp.where(lane < ROT_HALF,
                            pltpu.roll(x, LANES - ROT_HALF, 1),
                            pltpu.roll(x, ROT_HALF, 1))
        outs.append(x * cos + partner * sin)
    return jnp.concatenate(outs, axis=1)


def _proj_kernel(*refs, kind):
    if kind in ("rope_q", "rope_k"):
        xn_ref, w_ref, cos_ref, sin_ref = refs[:4]
        outs = refs[4:]
    elif kind == "gate":
        xn_ref, w_ref, w2_ref, b_ref = refs[:4]
        outs = refs[4:]
    else:
        xn_ref, w_ref = refs[:2]
        outs = refs[2:]
    y = _dot(xn_ref[...], w_ref[...])
    if kind == "rope_q":
        y = _rope(y, cos_ref[...], sin_ref[...]) * (DIFF_DK ** -0.5)
    elif kind == "rope_k":
        y = _rope(y, cos_ref[...], sin_ref[...])
    elif kind == "gla_qk":
        col = lax.broadcasted_iota(jnp.int32, y.shape, 1)
        y = jnp.where(col < GLA_QK, y * (GLA_DK ** -0.5), y)
    elif kind == "gate":
        z = _dot(y.astype(BF16), w2_ref[...]) + b_ref[...]
        y = (jnp.minimum(z, 0.0) - jnp.log1p(jnp.exp(-jnp.abs(z)))) / GLA_TAU
    for o in outs:
        o[...] = y.astype(o.dtype)


def _proj(xn, w, kind, out_dtypes, extra=()):
    n = xn.shape[0]
    tm = _tile(n, 1024)
    tn_in = w.shape[1]
    tn_out = GLA_QK if kind == "gate" else tn_in
    extra_specs = []
    for e in extra:
        if e.shape[0] == n:
            extra_specs.append(pl.BlockSpec((tm, e.shape[1]), lambda i: (i, 0)))
        else:
            extra_specs.append(pl.BlockSpec(e.shape, lambda i: (0, 0)))
    return pl.pallas_call(
        functools.partial(_proj_kernel, kind=kind),
        grid=(n // tm,),
        in_specs=[pl.BlockSpec((tm, D_MODEL), lambda i: (i, 0)),
                  pl.BlockSpec((D_MODEL, tn_in), lambda i: (0, 0))] + extra_specs,
        out_specs=[pl.BlockSpec((tm, tn_out), lambda i: (i, 0)) for _ in out_dtypes],
        out_shape=[jax.ShapeDtypeStruct((n, tn_out), dt) for dt in out_dtypes],
        compiler_params=_cparams(("parallel",)),
        name="proj_" + kind,
    )(xn, w, *extra)


def _lam(lq1, lk1, lq2, lk2):
    return (jnp.exp(jnp.sum(lq1 * lk1, axis=-1, keepdims=True))
            - jnp.exp(jnp.sum(lq2 * lk2, axis=-1, keepdims=True)) + LAM_INIT)


def _softmax_step(s, v, m_ref, l_ref, acc_ref):
    m_old = m_ref[...]
    m_new = jnp.maximum(m_old, jnp.max(s, axis=-1, keepdims=True))
    alpha = jnp.exp(m_old - m_new)
    p = jnp.exp(s - m_new)
    l_ref[...] = alpha * l_ref[...] + jnp.sum(p, axis=-1, keepdims=True)
    acc_ref[...] = alpha * acc_ref[...] + _dot(p.astype(BF16), v)
    m_ref[...] = m_new


def _attn_prompt_kernel(q_ref, k_ref, v_ref, lq1, lk1, lq2, lk2, g_ref, o_ref,
                        m_ref, l_ref, acc_ref, *, tq):
    qi = pl.program_id(1)
    m_ref[...] = jnp.full(m_ref.shape, NEG_INF, F32)
    l_ref[...] = jnp.zeros(l_ref.shape, F32)
    acc_ref[...] = jnp.zeros(acc_ref.shape, F32)
    q = q_ref[...]
    q1, q2 = q[:, :DIFF_DK], q[:, DIFF_DK:]

    def scores(kb):
        r0 = pl.multiple_of(kb * tq, tq)
        k = k_ref[pl.ds(r0, tq), :]
        s = jnp.concatenate([_dot_nt(q1, k[:, :DIFF_DK]), _dot_nt(q2, k[:, DIFF_DK:])], axis=0)
        return s, v_ref[pl.ds(r0, tq), :]

    def body(kb, carry):
        s, v = scores(kb)
        _softmax_step(s, v, m_ref, l_ref, acc_ref)
        return carry

    lax.fori_loop(0, qi, body, 0)
    s, v = scores(qi)
    row = lax.broadcasted_iota(jnp.int32, s.shape, 0) % tq
    col = lax.broadcasted_iota(jnp.int32, s.shape, 1)
    _softmax_step(jnp.where(col <= row, s, NEG_INF), v, m_ref, l_ref, acc_ref)

    o = acc_ref[...] / l_ref[...]
    lam = _lam(lq1[...], lk1[...], lq2[...], lk2[...])
    o = o[:tq] - lam * o[tq:]
    o_ref[...] = (_rms(o, g_ref[...]) * (1.0 - LAM_INIT)).astype(o_ref.dtype)


def _attn_prompt(q, k, v, lams, g_subln):
    t = q.shape[0]
    tq = _tile(t, 512)
    vec = pl.BlockSpec((1, DIFF_DK), lambda h, i: (0, 0))
    return pl.pallas_call(
        functools.partial(_attn_prompt_kernel, tq=tq),
        grid=(DIFF_HEADS, t // tq),
        in_specs=[pl.BlockSpec((tq, 2 * DIFF_DK), lambda h, i: (i, h)),
                  pl.BlockSpec((t, 2 * DIFF_DK), lambda h, i: (0, h)),
                  pl.BlockSpec((t, DIFF_DV), lambda h, i: (0, h)),
                  vec, vec, vec, vec,
                  pl.BlockSpec((1, DIFF_DV), lambda h, i: (0, 0))],
        out_specs=pl.BlockSpec((tq, DIFF_DV), lambda h, i: (i, h)),
        out_shape=jax.ShapeDtypeStruct((t, DIFF_V), BF16),
        scratch_shapes=[pltpu.VMEM((2 * tq, 1), F32), pltpu.VMEM((2 * tq, 1), F32),
                        pltpu.VMEM((2 * tq, DIFF_DV), F32)],
        compiler_params=_cparams(("parallel", "parallel")),
        name="attn_prompt",
    )(q, k, v, *lams, g_subln)


def _attn_sample_kernel(pt_ref, q_ref, kn_ref, vn_ref, *rest, td, pages):
    del pt_ref
    k_pages = rest[:pages]
    v_pages = rest[pages:2 * pages]
    lq1, lk1, lq2, lk2, g_ref, o_ref, qbd_ref, m_ref, l_ref, acc_ref = rest[2 * pages:]
    c = pl.program_id(1)
    rows = 2 * DIFF_HEADS * td

    @pl.when(c == 0)
    def _():
        m_ref[...] = jnp.full(m_ref.shape, NEG_INF, F32)
        l_ref[...] = jnp.zeros(l_ref.shape, F32)
        acc_ref[...] = jnp.zeros(acc_ref.shape, F32)
        q = jnp.concatenate([q_ref[...]] * (2 * DIFF_HEADS), axis=0)
        rid = lax.broadcasted_iota(jnp.int32, q.shape, 0) // td
        cid = lax.broadcasted_iota(jnp.int32, q.shape, 1) // DIFF_DK
        qbd_ref[...] = jnp.where(rid == cid, q, 0.0).astype(BF16)

    k = jnp.concatenate([r[...].astype(BF16) for r in k_pages], axis=0)
    v = jnp.concatenate([r[...].astype(BF16) for r in v_pages], axis=0)
    _softmax_step(_dot_nt(qbd_ref[...], k), v, m_ref, l_ref, acc_ref)

    @pl.when(c == pl.num_programs(1) - 1)
    def _():
        s = _dot_nt(qbd_ref[...], kn_ref[...].astype(BF16))
        t_q = lax.broadcasted_iota(jnp.int32, s.shape, 0) % td
        t_k = lax.broadcasted_iota(jnp.int32, s.shape, 1)
        _softmax_step(jnp.where(t_k <= t_q, s, NEG_INF), vn_ref[...].astype(BF16), m_ref, l_ref, acc_ref)
        o = acc_ref[...] / l_ref[...]
        lam = _lam(lq1[...], lk1[...], lq2[...], lk2[...])
        for h in range(DIFF_HEADS):
            cols = slice(h * DIFF_DV, (h + 1) * DIFF_DV)
            r1 = (2 * h) * td
            oh = o[r1:r1 + td, cols] - lam * o[r1 + td:r1 + 2 * td, cols]
            o_ref[:, cols] = (_rms(oh, g_ref[...]) * (1.0 - LAM_INIT)).astype(o_ref.dtype)
    del rows


def _attn_sample(q, k_new, v_new, cache_k, cache_v, page_table, lams, g_subln, td):
    nb, n_pages = page_table.shape
    page = cache_k.shape[1]
    pages = PAGES_PER_STEP
    assert n_pages % pages == 0
    rows = 2 * DIFF_HEADS * td

    def tok(b, c, pt):
        return (b, 0)

    def page_spec(i):
        return pl.BlockSpec((None, page, DIFF_QK),
                            lambda b, c, pt: (pt[b * n_pages + c * pages + i], 0, 0))

    const = lambda shape: pl.BlockSpec(shape, lambda b, c, pt: (0, 0))
    grid_spec = pltpu.PrefetchScalarGridSpec(
        num_scalar_prefetch=1,
        grid=(nb, n_pages // pages),
        in_specs=[pl.BlockSpec((td, DIFF_QK), tok), pl.BlockSpec((td, DIFF_QK), tok),
                  pl.BlockSpec((td, DIFF_V), tok)]
                 + [page_spec(i) for i in range(pages)] + [page_spec(i) for i in range(pages)]
                 + [const((1, DIFF_DK))] * 4 + [const((1, DIFF_DV))],
        out_specs=pl.BlockSpec((td, DIFF_V), tok),
        scratch_shapes=[pltpu.VMEM((rows, DIFF_QK), BF16), pltpu.VMEM((rows, 1), F32),
                        pltpu.VMEM((rows, 1), F32), pltpu.VMEM((rows, DIFF_V), F32)],
    )
    return pl.pallas_call(
        functools.partial(_attn_sample_kernel, td=td, pages=pages),
        grid_spec=grid_spec,
        out_shape=jax.ShapeDtypeStruct((nb * td, DIFF_V), F32),
        compiler_params=_cparams(("parallel", "arbitrary")),
        name="attn_sample",
    )(page_table.reshape(-1), q, k_new, v_new, *([cache_k] * pages), *([cache_v] * pages), *lams, g_subln)


def _gla_level_matrices(c):
    levels = c.bit_length() - 1
    t = lax.broadcasted_iota(jnp.int32, (c, c), 0)
    s = lax.broadcasted_iota(jnp.int32, (c, c), 1)
    tril = (s <= t).astype(F32)
    blocks, masks = [tril], []
    for l in range(1, levels + 1):
        grp, half = 1 << l, 1 << (l - 1)
        rho = (t // grp) * grp + half - 1
        blocks.append(tril - (s <= rho).astype(F32))
        masks.append(((t // grp) == (s // grp)) & ((t % grp) >= half) & ((s % grp) < half))
    return jnp.concatenate(blocks, axis=0), masks, (s == t)


def _split3_dot(m, g):
    g1 = g.astype(BF16)
    r1 = g - g1.astype(F32)
    g2 = r1.astype(BF16)
    g3 = (r1 - g2.astype(F32)).astype(BF16)
    mb = m.astype(BF16)
    return _dot(mb, g1) + _dot(mb, g2) + _dot(mb, g3)


def _gla_kernel(qk_ref, v_ref, la_ref, gr_ref, g_ref, s0_ref, o_ref, sfin_ref, s_ref, *, chunk, n_chunks):
    @pl.when(pl.program_id(1) == 0)
    def _():
        s_ref[...] = s0_ref[...]

    mstack, masks, eye = _gla_level_matrices(chunk)
    eye_dk = (lax.broadcasted_iota(jnp.int32, (GLA_DK, GLA_DK), 0)
              == lax.broadcasted_iota(jnp.int32, (GLA_DK, GLA_DK), 1))

    def do_chunk(ci, carry):
        r0 = pl.multiple_of(ci * chunk, chunk)
        rows = pl.ds(r0, chunk)
        for h in range(GLA_HEADS):
            kcols = slice(h * GLA_DK, (h + 1) * GLA_DK)
            vcols = slice(h * GLA_DV, (h + 1) * GLA_DV)
            q = qk_ref[rows, kcols]
            k = qk_ref[rows, GLA_QK + h * GLA_DK:GLA_QK + (h + 1) * GLA_DK]
            g = la_ref[rows, kcols]
            v = v_ref[rows, vcols].astype(BF16)
            state = s_ref[h]

            ex = _split3_dot(mstack, g)
            b = ex[:chunk]
            blast = b[chunk - 1:chunk]
            o = _dot((q * jnp.exp(b)).astype(BF16), state.astype(BF16))
            att = jnp.where(eye, _dot_nt(q.astype(BF16), k.astype(BF16)), 0.0)
            for l, mask in enumerate(masks):
                d = ex[(l + 1) * chunk:(l + 2) * chunk]
                a_l = (q * jnp.exp(jnp.minimum(d, 0.0))).astype(BF16)
                b_l = (k * jnp.exp(jnp.minimum(-d, 0.0))).astype(BF16)
                att = att + jnp.where(mask, _dot_nt(a_l, b_l), 0.0)
            o = o + _dot(att.astype(BF16), v)

            kdec = (k * jnp.exp(blast - b)).astype(BF16)
            dec_col = jnp.sum(jnp.where(eye_dk, jnp.exp(blast), 0.0), axis=1, keepdims=True)
            s_ref[h] = dec_col * state + _dot_tn(kdec, v)

            gate = _silu(gr_ref[rows, vcols])
            o_ref[rows, vcols] = (_rms(o, g_ref[...]) * gate).astype(o_ref.dtype)
        return carry

    lax.fori_loop(0, n_chunks, do_chunk, 0)

    @pl.when(pl.program_id(1) == pl.num_programs(1) - 1)
    def _():
        sfin_ref[...] = s_ref[...]


def _gla(qk, v, la, gr, g_gla, s0, nb, out_dtype):
    n = qk.shape[0]
    t = n // nb
    chunk = min(GLA_CHUNK, t)
    assert t % chunk == 0 and chunk & (chunk - 1) == 0
    tt = _tile(t, 512)
    n_t = t // tt
    row = lambda b, i: (b * n_t + i, 0)
    st = lambda b, i: (b, 0, 0, 0)
    return pl.pallas_call(
        functools.partial(_gla_kernel, chunk=chunk, n_chunks=tt // chunk),
        grid=(nb, n_t),
        in_specs=[pl.BlockSpec((tt, 2 * GLA_QK), row), pl.BlockSpec((tt, GLA_V), row),
                  pl.BlockSpec((tt, GLA_QK), row), pl.BlockSpec((tt, GLA_V), row),
                  pl.BlockSpec((1, GLA_DV), lambda b, i: (0, 0)),
                  pl.BlockSpec((None, GLA_HEADS, GLA_DK, GLA_DV), st)],
        out_specs=[pl.BlockSpec((tt, GLA_V), row),
                   pl.BlockSpec((None, GLA_HEADS, GLA_DK, GLA_DV), st)],
        out_shape=[jax.ShapeDtypeStruct((n, GLA_V), out_dtype),
                   jax.ShapeDtypeStruct((nb, GLA_HEADS, GLA_DK, GLA_DV), F32)],
        scratch_shapes=[pltpu.VMEM((GLA_HEADS, GLA_DK, GLA_DV), F32)],
        compiler_params=_cparams(("parallel", "arbitrary")),
        name="gla",
    )(qk, v, la, gr, g_gla, s0)


def _outproj_kernel(od_ref, og_ref, x_ref, w_ref, g_ref, h_ref, hn_ref):
    h = (x_ref[...] + _dot(od_ref[...].astype(BF16), w_ref[:DIFF_V, :])
         + _dot(og_ref[...].astype(BF16), w_ref[DIFF_V:, :]))
    h_ref[...] = h
    hn_ref[...] = _rms(h, g_ref[...]).astype(hn_ref.dtype)


def _outproj(od, og, x, w_out, g_ffn):
    n = x.shape[0]
    tm = _tile(n, 512)
    return pl.pallas_call(
        _outproj_kernel,
        grid=(n // tm,),
        in_specs=[pl.BlockSpec((tm, DIFF_V), lambda i: (i, 0)), pl.BlockSpec((tm, GLA_V), lambda i: (i, 0)),
                  pl.BlockSpec((tm, D_MODEL), lambda i: (i, 0)),
                  pl.BlockSpec((D_MODEL, D_MODEL), lambda i: (0, 0)),
                  pl.BlockSpec((1, D_MODEL), lambda i: (0, 0))],
        out_specs=[pl.BlockSpec((tm, D_MODEL), lambda i: (i, 0)), pl.BlockSpec((tm, D_MODEL), lambda i: (i, 0))],
        out_shape=[jax.ShapeDtypeStruct((n, D_MODEL), F32), jax.ShapeDtypeStruct((n, D_MODEL), BF16)],
        compiler_params=_cparams(("parallel",)),
        name="outproj",
    )(od, og, x, w_out, g_ffn)


def _ffn_kernel(hn_ref, h_ref, wg_ref, wv_ref, wc_ref, bc_ref, wd_ref, gf_ref, p1_ref, p2_ref,
                y_ref, gate_ref, acc_ref, carry_ref, *, seq, tf):
    i, j = pl.program_id(0), pl.program_id(1)
    hn = hn_ref[...]
    gate = _dot(hn, wg_ref[...])
    val = _dot(hn, wv_ref[...])
    tm = gate.shape[0]
    row = lax.broadcasted_iota(jnp.int32, gate.shape, 0)
    g1 = pltpu.roll(gate, 1, 0)
    g2 = pltpu.roll(gate, 2, 0)
    if seq:
        t = row % seq
        g1 = jnp.where(t == 0, p1_ref[...], g1)
        g2 = jnp.where(t < 2, p2_ref[...], g2)
        gate_ref[...] = gate
    else:
        cols = pl.ds(pl.multiple_of(j * tf, tf), tf)

        @pl.when(i == 0)
        def _():
            carry_ref[:, cols] = p1_ref[...]
        prev = carry_ref[:, cols]
        last, last2 = prev[SUBLANES - 1:SUBLANES], prev[SUBLANES - 2:SUBLANES - 1]
        g1 = jnp.where(row == 0, last, g1)
        g2 = jnp.where(row == 0, last2, jnp.where(row == 1, last, g2))
        tail = gate[tm - SUBLANES:]
        carry_ref[:, cols] = tail
        gate_ref[...] = tail
    wc = wc_ref[...]
    conv = bc_ref[...] + wc[0:1] * g2 + wc[1:2] * g1 + wc[2:3] * gate
    u = (_silu(conv) * val).astype(BF16)
    part = _dot(u, wd_ref[...])

    @pl.when(j == 0)
    def _():
        acc_ref[...] = part

    @pl.when(j > 0)
    def _():
        acc_ref[...] += part

    @pl.when(j == pl.num_programs(1) - 1)
    def _():
        y_ref[...] = _rms(h_ref[...] + acc_ref[...], gf_ref[...])


def _ffn(hn, h, w_up, w_conv, b_conv, w_down, g_final, p1, p2, seq):
    n = hn.shape[0]
    tm = _tile(n, 512)
    tf = 512
    assert D_FF % tf == 0
    n_j = D_FF // tf
    if seq:
        assert n == tm and tm % seq == 0
        prev_spec = pl.BlockSpec((tm, tf), lambda i, j: (0, j))
        gate_rows = tm
    else:
        prev_spec = pl.BlockSpec((SUBLANES, tf), lambda i, j: (0, j))
        gate_rows = SUBLANES
    return pl.pallas_call(
        functools.partial(_ffn_kernel, seq=seq, tf=tf),
        grid=(n // tm, n_j),
        in_specs=[pl.BlockSpec((tm, D_MODEL), lambda i, j: (i, 0)),
                  pl.BlockSpec((tm, D_MODEL), lambda i, j: (i, 0)),
                  pl.BlockSpec((D_MODEL, tf), lambda i, j: (0, j)),
                  pl.BlockSpec((D_MODEL, tf), lambda i, j: (0, n_j + j)),
                  pl.BlockSpec((CONV_W, tf), lambda i, j: (0, j)),
                  pl.BlockSpec((1, tf), lambda i, j: (0, j)),
                  pl.BlockSpec((tf, D_MODEL), lambda i, j: (j, 0)),
                  pl.BlockSpec((1, D_MODEL), lambda i, j: (0, 0)),
                  prev_spec, prev_spec],
        out_specs=[pl.BlockSpec((tm, D_MODEL), lambda i, j: (i, 0)),
                   pl.BlockSpec((gate_rows, tf), lambda i, j: (i, j))],
        out_shape=[jax.ShapeDtypeStruct((n, D_MODEL), F32),
                   jax.ShapeDtypeStruct((n // tm * gate_rows, D_FF), F32)],
        scratch_shapes=[pltpu.VMEM((tm, D_MODEL), F32), pltpu.VMEM((SUBLANES, D_FF), F32)],
        compiler_params=_cparams(("arbitrary", "arbitrary")),
        name="ffn",
    )(hn, h, w_up, w_up, w_conv, b_conv, w_down, g_final, p1, p2)


def _mixer_inputs(x, pos, wts, sample):
    xn, cos, sin = _norm(x, wts["g_mix"], pos, wts["freq_lane"])
    w = wts["w_in"]
    o = 0
    cols = {}
    for name, width in (("dq", DIFF_QK), ("dk", DIFF_QK), ("dv", DIFF_V), ("gqk", 2 * GLA_QK),
                        ("gv", GLA_V), ("gr", GLA_V), ("glr", GLA_RANK)):
        cols[name] = w[:, o:o + width]
        o += width
    act = F32 if sample else BF16
    (dq,) = _proj(xn, cols["dq"], "rope_q", (act,), (cos, sin))
    dk = _proj(xn, cols["dk"], "rope_k", (F32,) if sample else (F32, BF16), (cos, sin))
    dv = _proj(xn, cols["dv"], "plain", (F32,) if sample else (F32, BF16))
    (gqk,) = _proj(xn, cols["gqk"], "gla_qk", (F32,))
    (gv,) = _proj(xn, cols["gv"], "plain", (act,))
    (gr,) = _proj(xn, cols["gr"], "plain", (F32,))
    (la,) = _proj(xn, cols["glr"], "gate", (F32,), (wts["w_g2"], wts["b_g"]))
    return dq, dk, dv, gqk, gv, gr, la


def kernel(x_prompt, x_sample, cache_k, cache_v, state_gla, state_ffn_conv, page_table, g_mix, w_in, lam_q1, lam_k1, lam_q2, lam_k2, g_subln, w_g2, b_g, g_gla, w_out, g_ffn, w_up, w_conv, b_conv, w_down, g_final):
    nbp, t, _ = x_prompt.shape
    nbs, td, _ = x_sample.shape
    assert nbp == 1 and g_mix.shape[0] == 1
    n_phys, page = cache_k.shape[1], cache_k.shape[2]
    past_len = page_table.shape[1] * page

    inv_freq = ROPE_THETA ** (-jnp.arange(ROT_HALF, dtype=F32) * (2.0 / ROT_DIM))
    freq_lane = jnp.concatenate([inv_freq, inv_freq, jnp.zeros((LANES - ROT_DIM,), F32)]).reshape(1, LANES)
    row2 = lambda a: a[0].reshape(1, -1)
    wts = dict(g_mix=row2(g_mix), freq_lane=freq_lane, w_in=w_in[0].astype(BF16),
               w_g2=w_g2[0].astype(BF16), b_g=row2(b_g))
    lams = tuple(row2(a) for a in (lam_q1, lam_k1, lam_q2, lam_k2))
    g_sub, g_gl, g_ff, g_fin = row2(g_subln), row2(g_gla), row2(g_ffn), g_final.reshape(1, D_MODEL)
    w_out_b, w_up_b, w_down_b = w_out[0].astype(BF16), w_up[0].astype(BF16), w_down[0].astype(BF16)
    w_conv0, b_conv0 = w_conv[0], row2(b_conv)

    xp = x_prompt.reshape(t, D_MODEL)
    pos_p = jnp.arange(t, dtype=jnp.int32).astype(F32).reshape(t, 1)
    dq, (dk, dk_b), (dv, dv_b), gqk, gv, gr, la = _mixer_inputs(xp, pos_p, wts, sample=False)
    od = _attn_prompt(dq, dk_b, dv_b, lams, g_sub)
    og, s_p = _gla(gqk, gv, la, gr, g_gl, jnp.zeros((1, GLA_HEADS, GLA_DK, GLA_DV), F32), 1, BF16)
    h, hn = _outproj(od, og, xp, w_out_b, g_ff)
    buf0 = jnp.zeros((SUBLANES, D_FF), F32)
    y_p, tail = _ffn(hn, h, w_up_b, w_conv0, b_conv0, w_down_b, g_fin, buf0, buf0, seq=0)
    conv_p = tail[tail.shape[0] - (CONV_W - 1):]

    ns = nbs * td
    xs = x_sample.reshape(ns, D_MODEL)
    pos_s = jnp.tile(past_len + jnp.arange(td, dtype=jnp.int32), nbs).astype(F32).reshape(ns, 1)
    dq_s, (dk_s,), (dv_s,), gqk_s, gv_s, gr_s, la_s = _mixer_inputs(xs, pos_s, wts, sample=True)
    ck = cache_k[0].reshape(n_phys, page, DIFF_QK)
    cv = cache_v[0].reshape(n_phys, page, DIFF_V)
    od_s = _attn_sample(dq_s, dk_s, dv_s, ck, cv, page_table, lams, g_sub, td)
    og_s, s_s = _gla(gqk_s, gv_s, la_s, gr_s, g_gl, state_gla[0], nbs, F32)
    h_s, hn_s = _outproj(od_s, og_s, xs, w_out_b, g_ff)
    buf = state_ffn_conv[0]
    zeros = jnp.zeros((nbs, td, D_FF), F32)
    p1 = zeros.at[:, 0].set(buf[:, 1]).reshape(ns, D_FF)
    p2 = zeros.at[:, 0].set(buf[:, 0]).at[:, 1].set(buf[:, 1]).reshape(ns, D_FF)
    y_s, gate_s = _ffn(hn_s, h_s, w_up_b, w_conv0, b_conv0, w_down_b, g_fin, p1, p2, seq=td)
    conv_s = jnp.concatenate([buf, gate_s.reshape(nbs, td, D_FF)], axis=1)[:, td:]

    return (y_p.reshape(1, t, D_MODEL), y_s.reshape(nbs, td, D_MODEL),
            dk.reshape(1, 1, t, DIFF_HEADS, 2, DIFF_DK), dv.reshape(1, 1, t, DIFF_HEADS, DIFF_DV),
            s_p.reshape(1, 1, GLA_HEADS, GLA_DK, GLA_DV), conv_p.reshape(1, 1, CONV_W - 1, D_FF),
            dk_s.reshape(1, nbs, td, DIFF_HEADS, 2, DIFF_DK), dv_s.reshape(1, nbs, td, DIFF_HEADS, DIFF_DV),
            s_s.reshape(1, nbs, GLA_HEADS, GLA_DK, GLA_DV), conv_s.reshape(1, nbs, CONV_W - 1, D_FF))
```

```python
import functools
import math

import jax
import jax.numpy as jnp
from jax import lax
from jax.experimental import pallas as pl
from jax.experimental.pallas import tpu as pltpu

F32 = jnp.float32
BF16 = jnp.bfloat16

D_MODEL = 2048
DIFF_HEADS = 4
DIFF_DK = 128
DIFF_DV = 256
GLA_HEADS = 4
GLA_DK = 128
GLA_DV = 256
GLA_RANK = 16
GLA_TAU = 16.0
GLA_CHUNK = 64
ROPE_THETA = 500000.0
ROT_DIM = DIFF_DK // 4
ROT_HALF = ROT_DIM // 2
D_FF = 5632
CONV_W = 3
EPS = 1e-6
LAM_INIT = 0.8 - 0.6 * math.exp(-0.3 * 0)
LOG2_E = math.log2(math.e)

DIFF_QK = DIFF_HEADS * 2 * DIFF_DK
DIFF_V = DIFF_HEADS * DIFF_DV
GLA_QK = GLA_HEADS * GLA_DK
GLA_V = GLA_HEADS * GLA_DV
LANES = 128
SUBLANES = 8

VMEM_LIMIT = 56 * 1024 * 1024

PAGES_PER_STEP = 8
NEG_INF = float("-inf")


def _cparams(sem):
    return pltpu.CompilerParams(dimension_semantics=sem, vmem_limit_bytes=VMEM_LIMIT)


def _tile(n, pref):
    t = min(n, pref)
    assert n % t == 0, (n, pref)
    return t


def _rms(x, g):
    return x * lax.rsqrt(jnp.mean(x * x, axis=-1, keepdims=True) + EPS) * g


def _silu(x):
    return x * (1.0 / (1.0 + jnp.exp(-x)))


def _dot(a, b):
    return jnp.dot(a, b, preferred_element_type=F32)


def _dot_nt(a, b):
    return lax.dot_general(a, b, (((1,), (1,)), ((), ())), preferred_element_type=F32)


def _dot_tn(a, b):
    return lax.dot_general(a, b, (((0,), (0,)), ((), ())), preferred_element_type=F32)


def _norm_kernel(x_ref, g_ref, pos_ref, freq_ref, xn_ref, cos_ref, sin_ref):
    xn_ref[...] = _rms(x_ref[...], g_ref[...]).astype(xn_ref.dtype)
    ang = pos_ref[...] * freq_ref[...]
    lane = lax.broadcasted_iota(jnp.int32, ang.shape, 1)
    s = jnp.sin(ang)
    cos_ref[...] = jnp.cos(ang)
    sin_ref[...] = jnp.where(lane < ROT_HALF, -s, s)


def _norm(x, g, pos, freq_lane):
    n = x.shape[0]
    tm = _tile(n, 512)
    return pl.pallas_call(
        _norm_kernel,
        grid=(n // tm,),
        in_specs=[pl.BlockSpec((tm, D_MODEL), lambda i: (i, 0)),
                  pl.BlockSpec((1, D_MODEL), lambda i: (0, 0)),
                  pl.BlockSpec((tm, 1), lambda i: (i, 0)),
                  pl.BlockSpec((1, LANES), lambda i: (0, 0))],
        out_specs=[pl.BlockSpec((tm, D_MODEL), lambda i: (i, 0)),
                   pl.BlockSpec((tm, LANES), lambda i: (i, 0)),
                   pl.BlockSpec((tm, LANES), lambda i: (i, 0))],
        out_shape=[jax.ShapeDtypeStruct((n, D_MODEL), BF16),
                   jax.ShapeDtypeStruct((n, LANES), F32),
                   jax.ShapeDtypeStruct((n, LANES), F32)],
        compiler_params=_cparams(("parallel",)),
        name="norm",
    )(x, g, pos, freq_lane)


def _rope(y, cos, sin):
    lane = lax.broadcasted_iota(jnp.int32, cos.shape, 1)
    outs = []
    for s in range(y.shape[1] // LANES):
        x = y[:, s * LANES:(s + 1) * LANES]
        partner = jnp.where(lane < ROT_HALF,
                            pltpu.roll(x, LANES - ROT_HALF, 1),
                            pltpu.roll(x, ROT_HALF, 1))
        outs.append(x * cos + partner * sin)
    return jnp.concatenate(outs, axis=1)


def _store_cache_rows(o_ref, y, kind):
    tm = y.shape[0]
    n_rows = DIFF_QK // LANES
    for s in range(n_rows):
        if kind == "rope_k":
            dst = s
        else:
            h, half = divmod(s, DIFF_DV // LANES)
            dst = half * DIFF_HEADS + h
        o_ref[pl.ds(dst, tm, stride=n_rows), :] = y[:, s * LANES:(s + 1) * LANES]


def _proj_kernel(*refs, kind, n_std):
    wb_ref = refs[-1]
    refs = refs[:-1]
    if kind in ("rope_q", "rope_k"):
        xn_ref, w_ref, cos_ref, sin_ref = refs[:4]
        outs = refs[4:]
    elif kind == "gate":
        xn_ref, w_ref, w2_ref, b_ref = refs[:4]
        outs = refs[4:]
    else:
        xn_ref, w_ref = refs[:2]
        outs = refs[2:]

    @pl.when(pl.program_id(0) == 0)
    def _():
        wb_ref[...] = w_ref[...].T.astype(BF16)

    y = _dot(xn_ref[...], wb_ref[...])
    if kind == "rope_q":
        y = _rope(y, cos_ref[...], sin_ref[...]) * (DIFF_DK ** -0.5 * LOG2_E)
    elif kind == "rope_k":
        y = _rope(y, cos_ref[...], sin_ref[...])
    elif kind == "gla_qk":
        col = lax.broadcasted_iota(jnp.int32, y.shape, 1)
        y = jnp.where(col < GLA_QK, y * (GLA_DK ** -0.5), y)
    elif kind == "gate":
        z = _dot(y.astype(BF16), w2_ref[...].astype(BF16)) + b_ref[...]
        y = (jnp.minimum(z, 0.0) - jnp.log1p(jnp.exp(-jnp.abs(z)))) / GLA_TAU
    for o in outs[:n_std]:
        o[...] = y.astype(o.dtype)
    for o in outs[n_std:]:
        _store_cache_rows(o, y, kind)


def _proj(xn, wt, col_block, width, kind, out_dtypes, extra=(), cache_rows=False):
    n = xn.shape[0]
    tm = _tile(n, 1024)
    tn_out = GLA_QK if kind == "gate" else width
    extra_specs = []
    for e in extra:
        if e.shape[0] == n:
            extra_specs.append(pl.BlockSpec((tm, e.shape[1]), lambda i: (i, 0)))
        else:
            extra_specs.append(pl.BlockSpec(e.shape, lambda i: (0, 0)))
    out_specs = [pl.BlockSpec((tm, tn_out), lambda i: (i, 0)) for _ in out_dtypes]
    out_shape = [jax.ShapeDtypeStruct((n, tn_out), dt) for dt in out_dtypes]
    if cache_rows:
        per_tok = width // LANES
        out_specs.append(pl.BlockSpec((tm * per_tok, LANES), lambda i: (i, 0)))
        out_shape.append(jax.ShapeDtypeStruct((n * per_tok, LANES), F32))
    return pl.pallas_call(
        functools.partial(_proj_kernel, kind=kind, n_std=len(out_dtypes)),
        grid=(n // tm,),
        in_specs=[pl.BlockSpec((tm, D_MODEL), lambda i: (i, 0)),
                  pl.BlockSpec((width, D_MODEL), lambda i: (col_block, 0), pipeline_mode=pl.Buffered(1))]
                 + extra_specs,
        out_specs=out_specs,
        out_shape=out_shape,
        scratch_shapes=[pltpu.VMEM((D_MODEL, width), BF16)],
        compiler_params=_cparams(("arbitrary",)),
        name="proj_" + kind,
    )(xn, wt, *extra)


def _lam(lq1, lk1, lq2, lk2):
    return (jnp.exp(jnp.sum(lq1 * lk1, axis=-1, keepdims=True))
            - jnp.exp(jnp.sum(lq2 * lk2, axis=-1, keepdims=True)) + LAM_INIT)


def _attn_prompt_kernel(q_ref, k_ref, v_ref, lq1, lk1, lq2, lk2, g_ref, o_ref,
                        m_ref, l_ref, acc_ref, *, tq, rb):
    qi = pl.program_id(1)
    m_ref[...] = jnp.full(m_ref.shape, NEG_INF, F32)
    l_ref[...] = jnp.zeros(l_ref.shape, F32)
    acc_ref[...] = jnp.zeros(acc_ref.shape, F32)

    def block(kb, diagonal):
        r0 = pl.multiple_of(kb * tq, tq)
        k = k_ref[pl.ds(r0, tq), :]
        v = v_ref[pl.ds(r0, tq), :]
        for r in range(2 * tq // rb):
            amap, q0 = divmod(r * rb, tq)
            rows = slice(r * rb, (r + 1) * rb)
            dcols = slice(amap * DIFF_DK, (amap + 1) * DIFF_DK)
            nk = tq
            s = _dot_nt(q_ref[q0:q0 + rb, dcols], k[:nk, dcols])
            if diagonal:
                row = q0 + lax.broadcasted_iota(jnp.int32, s.shape, 0)
                col = lax.broadcasted_iota(jnp.int32, s.shape, 1)
                s = jnp.where(col <= row, s, NEG_INF)
            chunks = [s[:, c * LANES:(c + 1) * LANES] for c in range(nk // LANES)]
            mx = functools.reduce(jnp.maximum, chunks)
            m_old = m_ref[rows]
            m_new = jnp.maximum(m_old, jnp.max(mx, axis=-1, keepdims=True))
            alpha = jnp.exp2(m_old - m_new)
            ps = [jnp.exp2(c - m_new) for c in chunks]
            l_ref[rows] = alpha * l_ref[rows] + functools.reduce(jnp.add, ps)
            pv = _dot(jnp.concatenate(ps, axis=1).astype(BF16), v[:nk])
            acc_ref[rows] = jnp.concatenate([alpha] * (DIFF_DV // LANES), axis=1) * acc_ref[rows] + pv
            m_ref[rows] = m_new

    def body(kb, carry):
        block(kb, False)
        return carry

    lax.fori_loop(0, qi, body, 0)
    block(qi, True)

    o = acc_ref[...] / jnp.sum(l_ref[...], axis=-1, keepdims=True)
    lam = _lam(lq1[...], lk1[...], lq2[...], lk2[...])
    o = o[:tq] - lam * o[tq:]
    o_ref[...] = (_rms(o, g_ref[...]) * (1.0 - LAM_INIT)).astype(o_ref.dtype)


def _attn_prompt(q, k, v, lams, g_subln):
    t = q.shape[0]
    tq = _tile(t, 512)
    rb = _tile(tq, 128)
    vec = pl.BlockSpec((1, DIFF_DK), lambda h, i: (0, 0))
    return pl.pallas_call(
        functools.partial(_attn_prompt_kernel, tq=tq, rb=rb),
        grid=(DIFF_HEADS, t // tq),
        in_specs=[pl.BlockSpec((tq, 2 * DIFF_DK), lambda h, i: (i, h)),
                  pl.BlockSpec((t, 2 * DIFF_DK), lambda h, i: (0, h)),
                  pl.BlockSpec((t, DIFF_DV), lambda h, i: (0, h)),
                  vec, vec, vec, vec,
                  pl.BlockSpec((1, DIFF_DV), lambda h, i: (0, 0))],
        out_specs=pl.BlockSpec((tq, DIFF_DV), lambda h, i: (i, h)),
        out_shape=jax.ShapeDtypeStruct((t, DIFF_V), BF16),
        scratch_shapes=[pltpu.VMEM((2 * tq, LANES), F32), pltpu.VMEM((2 * tq, LANES), F32),
                        pltpu.VMEM((2 * tq, DIFF_DV), F32)],
        compiler_params=_cparams(("parallel", "parallel")),
        name="attn_prompt",
    )(q, k, v, *lams, g_subln)


def _attn_sample_kernel(pt_ref, q_ref, kn_ref, vn_ref, *rest, td, pages, page):
    del pt_ref
    k_pages = rest[:pages]
    v_pages = rest[pages:2 * pages]
    lq1, lk1, lq2, lk2, g_ref, o_ref, m_ref, l_ref, acc_ref = rest[2 * pages:]
    c = pl.program_id(1)
    n_hm = 2 * DIFF_HEADS
    halves = DIFF_DV // LANES

    @pl.when(c == 0)
    def _():
        m_ref[...] = jnp.full(m_ref.shape, NEG_INF, F32)
        l_ref[...] = jnp.zeros(l_ref.shape, F32)
        acc_ref[...] = jnp.zeros(acc_ref.shape, F32)

    q = q_ref[...].astype(BF16)

    def q_hm(hm):
        return q[:, hm * DIFF_DK:(hm + 1) * DIFF_DK]

    def update(s, v_of_head):
        m_old = m_ref[...]
        m_new = jnp.maximum(m_old, jnp.max(s, axis=-1, keepdims=True))
        alpha = jnp.exp2(m_old - m_new)
        p = jnp.exp2(s - m_new)
        l_ref[...] = alpha * l_ref[...] + jnp.sum(p, axis=-1, keepdims=True)
        pb = p.astype(BF16)
        pv = jnp.concatenate([_dot(pb[2 * h * td:2 * (h + 1) * td], v_of_head(h))
                              for h in range(DIFF_HEADS)], axis=0)
        acc_ref[...] = alpha * acc_ref[...] + pv
        m_ref[...] = m_new

    def k_rows(refs, n_tok, hm):
        return jnp.concatenate([r[pl.ds(hm, n_tok, stride=n_hm), :].astype(BF16) for r in refs], axis=0)

    def v_rows(refs, n_tok, h):
        return jnp.concatenate(
            [jnp.concatenate([r[pl.ds(half * DIFF_HEADS + h, n_tok, stride=halves * DIFF_HEADS), :].astype(BF16)
                              for half in range(halves)], axis=1) for r in refs], axis=0)

    update(jnp.concatenate([_dot_nt(q_hm(hm), k_rows(k_pages, page, hm)) for hm in range(n_hm)], axis=0),
           functools.partial(v_rows, v_pages, page))

    @pl.when(c == pl.num_programs(1) - 1)
    def _():
        s = jnp.concatenate([_dot_nt(q_hm(hm), k_rows([kn_ref], td, hm)) for hm in range(n_hm)], axis=0)
        t_q = lax.broadcasted_iota(jnp.int32, s.shape, 0) % td
        t_k = lax.broadcasted_iota(jnp.int32, s.shape, 1)
        update(jnp.where(t_k <= t_q, s, NEG_INF), functools.partial(v_rows, [vn_ref], td))
        o = acc_ref[...] / l_ref[...]
        lam = _lam(lq1[...], lk1[...], lq2[...], lk2[...])
        for h in range(DIFF_HEADS):
            r1 = 2 * h * td
            oh = o[r1:r1 + td] - lam * o[r1 + td:r1 + 2 * td]
            o_ref[:, h * DIFF_DV:(h + 1) * DIFF_DV] = (_rms(oh, g_ref[...]) * (1.0 - LAM_INIT)).astype(o_ref.dtype)


def _attn_sample(q, k_new, v_new, cache_k, cache_v, page_table, lams, g_subln, td, page):
    nb, n_pages = page_table.shape
    pages = PAGES_PER_STEP
    assert n_pages % pages == 0
    rows = 2 * DIFF_HEADS * td
    page_rows = page * DIFF_QK // LANES

    def tok(b, c, pt):
        return (b, 0)

    def page_spec(i):
        return pl.BlockSpec((page_rows, LANES), lambda b, c, pt: (pt[b * n_pages + c * pages + i], 0))

    const = lambda shape: pl.BlockSpec(shape, lambda b, c, pt: (0, 0))
    grid_spec = pltpu.PrefetchScalarGridSpec(
        num_scalar_prefetch=1,
        grid=(nb, n_pages // pages),
        in_specs=[pl.BlockSpec((td, DIFF_QK), tok), pl.BlockSpec((td * DIFF_QK // LANES, LANES), tok),
                  pl.BlockSpec((td * DIFF_V // LANES, LANES), tok)]
                 + [page_spec(i) for i in range(pages)] + [page_spec(i) for i in range(pages)]
                 + [const((1, DIFF_DK))] * 4 + [const((1, DIFF_DV))],
        out_specs=pl.BlockSpec((td, DIFF_V), tok),
        scratch_shapes=[pltpu.VMEM((rows, 1), F32), pltpu.VMEM((rows, 1), F32),
                        pltpu.VMEM((rows, DIFF_DV), F32)],
    )
    return pl.pallas_call(
        functools.partial(_attn_sample_kernel, td=td, pages=pages, page=page),
        grid_spec=grid_spec,
        out_shape=jax.ShapeDtypeStruct((nb * td, DIFF_V), F32),
        compiler_params=_cparams(("parallel", "arbitrary")),
        name="attn_sample",
    )(page_table.reshape(-1), q, k_new, v_new, *([cache_k] * pages), *([cache_v] * pages), *lams, g_subln)


def _gla_level_matrices(c):
    levels = c.bit_length() - 1
    t = lax.broadcasted_iota(jnp.int32, (c, c), 0)
    s = lax.broadcasted_iota(jnp.int32, (c, c), 1)
    tril = (s <= t).astype(F32)
    blocks, masks = [tril], []
    for l in range(1, levels + 1):
        grp, half = 1 << l, 1 << (l - 1)
        rho = (t // grp) * grp + half - 1
        blocks.append(tril - (s <= rho).astype(F32))
        masks.append(((t // grp) == (s // grp)) & ((t % grp) >= half) & ((s % grp) < half))
    return jnp.concatenate(blocks, axis=0), masks, (s == t)


def _split3_dot(m, g):
    g1 = g.astype(BF16)
    r1 = g - g1.astype(F32)
    g2 = r1.astype(BF16)
    g3 = (r1 - g2.astype(F32)).astype(BF16)
    mb = m.astype(BF16)
    return _dot(mb, g1) + _dot(mb, g2) + _dot(mb, g3)


def _gla_kernel(qk_ref, v_ref, la_ref, gr_ref, g_ref, s0_ref, o_ref, sfin_ref, s_ref, *, chunk, n_chunks):
    @pl.when(pl.program_id(1) == 0)
    def _():
        s_ref[...] = s0_ref[...]

    mstack, masks, eye = _gla_level_matrices(chunk)
    eye_dk = (lax.broadcasted_iota(jnp.int32, (GLA_DK, GLA_DK), 0)
              == lax.broadcasted_iota(jnp.int32, (GLA_DK, GLA_DK), 1))

    def do_chunk(ci, carry):
        r0 = pl.multiple_of(ci * chunk, chunk)
        rows = pl.ds(r0, chunk)
        for h in range(GLA_HEADS):
            kcols = slice(h * GLA_DK, (h + 1) * GLA_DK)
            vcols = slice(h * GLA_DV, (h + 1) * GLA_DV)
            q = qk_ref[rows, kcols]
            k = qk_ref[rows, GLA_QK + h * GLA_DK:GLA_QK + (h + 1) * GLA_DK]
            g = la_ref[rows, kcols]
            v = v_ref[rows, vcols].astype(BF16)
            state = s_ref[h]

            ex = _split3_dot(mstack, g)
            b = ex[:chunk]
            blast = b[chunk - 1:chunk]
            o = _dot((q * jnp.exp(b)).astype(BF16), state.astype(BF16))
            att = jnp.where(eye, _dot_nt(q.astype(BF16), k.astype(BF16)), 0.0)
            for l, mask in enumerate(masks):
                d = ex[(l + 1) * chunk:(l + 2) * chunk]
                a_l = (q * jnp.exp(jnp.minimum(d, 0.0))).astype(BF16)
                b_l = (k * jnp.exp(jnp.minimum(-d, 0.0))).astype(BF16)
                att = att + jnp.where(mask, _dot_nt(a_l, b_l), 0.0)
            o = o + _dot(att.astype(BF16), v)

            kdec = (k * jnp.exp(blast - b)).astype(BF16)
            dec_col = jnp.sum(jnp.where(eye_dk, jnp.exp(blast), 0.0), axis=1, keepdims=True)
            s_ref[h] = dec_col * state + _dot_tn(kdec, v)

            gate = _silu(gr_ref[rows, vcols])
            o_ref[rows, vcols] = (_rms(o, g_ref[...]) * gate).astype(o_ref.dtype)
        return carry

    lax.fori_loop(0, n_chunks, do_chunk, 0)

    @pl.when(pl.program_id(1) == pl.num_programs(1) - 1)
    def _():
        sfin_ref[...] = s_ref[...]


def _gla(qk, v, la, gr, g_gla, s0, nb, out_dtype):
    n = qk.shape[0]
    t = n // nb
    chunk = min(GLA_CHUNK, t)
    assert t % chunk == 0 and chunk & (chunk - 1) == 0
    tt = _tile(t, 512)
    n_t = t // tt
    row = lambda b, i: (b * n_t + i, 0)
    st = lambda b, i: (b, 0, 0, 0)
    return pl.pallas_call(
        functools.partial(_gla_kernel, chunk=chunk, n_chunks=tt // chunk),
        grid=(nb, n_t),
        in_specs=[pl.BlockSpec((tt, 2 * GLA_QK), row), pl.BlockSpec((tt, GLA_V), row),
                  pl.BlockSpec((tt, GLA_QK), row), pl.BlockSpec((tt, GLA_V), row),
                  pl.BlockSpec((1, GLA_DV), lambda b, i: (0, 0)),
                  pl.BlockSpec((None, GLA_HEADS, GLA_DK, GLA_DV), st)],
        out_specs=[pl.BlockSpec((tt, GLA_V), row),
                   pl.BlockSpec((None, GLA_HEADS, GLA_DK, GLA_DV), st)],
        out_shape=[jax.ShapeDtypeStruct((n, GLA_V), out_dtype),
                   jax.ShapeDtypeStruct((nb, GLA_HEADS, GLA_DK, GLA_DV), F32)],
        scratch_shapes=[pltpu.VMEM((GLA_HEADS, GLA_DK, GLA_DV), F32)],
        compiler_params=_cparams(("parallel", "arbitrary")),
        name="gla",
    )(qk, v, la, gr, g_gla, s0)


def _outproj_kernel(od_ref, og_ref, x_ref, w_ref, g_ref, h_ref, hn_ref):
    h = (x_ref[...] + _dot(od_ref[...].astype(BF16), w_ref[:DIFF_V, :])
         + _dot(og_ref[...].astype(BF16), w_ref[DIFF_V:, :]))
    h_ref[...] = h
    hn_ref[...] = _rms(h, g_ref[...]).astype(hn_ref.dtype)


def _outproj(od, og, x, w_out, g_ffn):
    n = x.shape[0]
    tm = _tile(n, 512)
    return pl.pallas_call(
        _outproj_kernel,
        grid=(n // tm,),
        in_specs=[pl.BlockSpec((tm, DIFF_V), lambda i: (i, 0)), pl.BlockSpec((tm, GLA_V), lambda i: (i, 0)),
                  pl.BlockSpec((tm, D_MODEL), lambda i: (i, 0)),
                  pl.BlockSpec((D_MODEL, D_MODEL), lambda i: (0, 0)),
                  pl.BlockSpec((1, D_MODEL), lambda i: (0, 0))],
        out_specs=[pl.BlockSpec((tm, D_MODEL), lambda i: (i, 0)), pl.BlockSpec((tm, D_MODEL), lambda i: (i, 0))],
        out_shape=[jax.ShapeDtypeStruct((n, D_MODEL), F32), jax.ShapeDtypeStruct((n, D_MODEL), BF16)],
        compiler_params=_cparams(("parallel",)),
        name="outproj",
    )(od, og, x, w_out, g_ffn)


def _ffn_kernel(hn_ref, h_ref, wg_ref, wv_ref, wc_ref, bc_ref, wd_ref, gf_ref, p1_ref, p2_ref,
                y_ref, gate_ref, acc_ref, carry_ref, *, seq, tf, rb):
    i, j = pl.program_id(0), pl.program_id(1)
    tm = hn_ref.shape[0]

    @pl.when(j == 0)
    def _():
        acc_ref[...] = jnp.zeros(acc_ref.shape, F32)

    if not seq:
        cols = pl.ds(pl.multiple_of(j * tf, tf), tf)

        @pl.when(i == 0)
        def _():
            carry_ref[:, cols] = p1_ref[...]
        prev = carry_ref[:, cols]
    wc = wc_ref[...]
    for r in range(tm // rb):
        rows = slice(r * rb, (r + 1) * rb)
        hn = hn_ref[rows, :]
        gate = _dot(hn, wg_ref[...])
        val = _dot(hn, wv_ref[...])
        row = lax.broadcasted_iota(jnp.int32, gate.shape, 0)
        g1 = pltpu.roll(gate, 1, 0)
        g2 = pltpu.roll(gate, 2, 0)
        if seq:
            t = row % seq
            g1 = jnp.where(t == 0, p1_ref[rows, :], g1)
            g2 = jnp.where(t < 2, p2_ref[rows, :], g2)
            gate_ref[rows, :] = gate
        else:
            last, last2 = prev[SUBLANES - 1:SUBLANES], prev[SUBLANES - 2:SUBLANES - 1]
            g1 = jnp.where(row == 0, last, g1)
            g2 = jnp.where(row == 0, last2, jnp.where(row == 1, last, g2))
            prev = gate[rb - SUBLANES:]
        conv = bc_ref[...] + wc[0:1] * g2 + wc[1:2] * g1 + wc[2:3] * gate
        u = (_silu(conv) * val).astype(BF16)
        acc_ref[rows, :] += _dot(u, wd_ref[...])
    if not seq:
        carry_ref[:, cols] = prev
        gate_ref[...] = prev

    @pl.when(j == pl.num_programs(1) - 1)
    def _():
        y_ref[...] = _rms(h_ref[...] + acc_ref[...], gf_ref[...])


def _ffn(hn, h, w_up, w_conv, b_conv, w_down, g_final, p1, p2, seq):
    n = hn.shape[0]
    tm = _tile(n, 512)
    tf = 512
    assert D_FF % tf == 0
    n_j = D_FF // tf
    if seq:
        assert n == tm and tm % seq == 0
        prev_spec = pl.BlockSpec((tm, tf), lambda i, j: (0, j))
        gate_rows = tm
    else:
        prev_spec = pl.BlockSpec((SUBLANES, tf), lambda i, j: (0, j))
        gate_rows = SUBLANES
    return pl.pallas_call(
        functools.partial(_ffn_kernel, seq=seq, tf=tf, rb=_tile(tm, 256)),
        grid=(n // tm, n_j),
        in_specs=[pl.BlockSpec((tm, D_MODEL), lambda i, j: (i, 0)),
                  pl.BlockSpec((tm, D_MODEL), lambda i, j: (i, 0)),
                  pl.BlockSpec((D_MODEL, tf), lambda i, j: (0, j)),
                  pl.BlockSpec((D_MODEL, tf), lambda i, j: (0, n_j + j)),
                  pl.BlockSpec((CONV_W, tf), lambda i, j: (0, j)),
                  pl.BlockSpec((1, tf), lambda i, j: (0, j)),
                  pl.BlockSpec((tf, D_MODEL), lambda i, j: (j, 0)),
                  pl.BlockSpec((1, D_MODEL), lambda i, j: (0, 0)),
                  prev_spec, prev_spec],
        out_specs=[pl.BlockSpec((tm, D_MODEL), lambda i, j: (i, 0)),
                   pl.BlockSpec((gate_rows, tf), lambda i, j: (i, j))],
        out_shape=[jax.ShapeDtypeStruct((n, D_MODEL), F32),
                   jax.ShapeDtypeStruct((n // tm * gate_rows, D_FF), F32)],
        scratch_shapes=[pltpu.VMEM((tm, D_MODEL), F32), pltpu.VMEM((SUBLANES, D_FF), F32)],
        compiler_params=_cparams(("arbitrary", "arbitrary")),
        name="ffn",
    )(hn, h, w_up, w_up, w_conv, b_conv, w_down, g_final, p1, p2)


def _mixer_inputs(x, pos, wts, sample):
    xn, cos, sin = _norm(x, wts["g_mix"], pos, wts["freq_lane"])
    w = wts["w_in_t"]
    wide = DIFF_QK
    assert DIFF_V == wide and 2 * GLA_QK == wide and GLA_V == wide and wide % GLA_RANK == 0
    act = F32 if sample else BF16
    kv_std = () if sample else (BF16,)
    (dq,) = _proj(xn, w, 0, wide, "rope_q", (act,), (cos, sin))
    dk = _proj(xn, w, 1, wide, "rope_k", kv_std, (cos, sin), cache_rows=True)
    dv = _proj(xn, w, 2, wide, "plain", kv_std, cache_rows=True)
    (gqk,) = _proj(xn, w, 3, wide, "gla_qk", (F32,))
    (gv,) = _proj(xn, w, 4, wide, "plain", (act,))
    (gr,) = _proj(xn, w, 5, wide, "plain", (F32,))
    (la,) = _proj(xn, w, 6 * wide // GLA_RANK, GLA_RANK, "gate", (F32,), (wts["w_g2"], wts["b_g"]))
    return dq, dk, dv, gqk, gv, gr, la


def kernel(x_prompt, x_sample, cache_k, cache_v, state_gla, state_ffn_conv, page_table, g_mix, w_in, lam_q1, lam_k1, lam_q2, lam_k2, g_subln, w_g2, b_g, g_gla, w_out, g_ffn, w_up, w_conv, b_conv, w_down, g_final):
    nbp, t, _ = x_prompt.shape
    nbs, td, _ = x_sample.shape
    assert nbp == 1 and g_mix.shape[0] == 1
    n_phys, page = cache_k.shape[1], cache_k.shape[2]
    past_len = page_table.shape[1] * page

    inv_freq = ROPE_THETA ** (-jnp.arange(ROT_HALF, dtype=F32) * (2.0 / ROT_DIM))
    freq_lane = jnp.concatenate([inv_freq, inv_freq, jnp.zeros((LANES - ROT_DIM,), F32)]).reshape(1, LANES)
    row2 = lambda a: a[0].reshape(1, -1)
    wts = dict(g_mix=row2(g_mix), freq_lane=freq_lane, w_in_t=w_in[0].T, w_g2=w_g2[0], b_g=row2(b_g))
    lams = tuple(row2(a) for a in (lam_q1, lam_k1, lam_q2, lam_k2))
    g_sub, g_gl, g_ff, g_fin = row2(g_subln), row2(g_gla), row2(g_ffn), g_final.reshape(1, D_MODEL)
    w_out_b, w_up_b, w_down_b = w_out[0].astype(BF16), w_up[0].astype(BF16), w_down[0].astype(BF16)
    w_conv0, b_conv0 = w_conv[0], row2(b_conv)

    xp = x_prompt.reshape(t, D_MODEL)
    pos_p = jnp.arange(t, dtype=jnp.int32).astype(F32).reshape(t, 1)
    dq, (dk_b, dk), (dv_b, dv), gqk, gv, gr, la = _mixer_inputs(xp, pos_p, wts, sample=False)
    od = _attn_prompt(dq, dk_b, dv_b, lams, g_sub)
    og, s_p = _gla(gqk, gv, la, gr, g_gl, jnp.zeros((1, GLA_HEADS, GLA_DK, GLA_DV), F32), 1, BF16)
    h, hn = _outproj(od, og, xp, w_out_b, g_ff)
    buf0 = jnp.zeros((SUBLANES, D_FF), F32)
    y_p, tail = _ffn(hn, h, w_up_b, w_conv0, b_conv0, w_down_b, g_fin, buf0, buf0, seq=0)
    conv_p = tail[tail.shape[0] - (CONV_W - 1):]

    ns = nbs * td
    xs = x_sample.reshape(ns, D_MODEL)
    pos_s = jnp.tile(past_len + jnp.arange(td, dtype=jnp.int32), nbs).astype(F32).reshape(ns, 1)
    dq_s, (dk_s,), (dv_s,), gqk_s, gv_s, gr_s, la_s = _mixer_inputs(xs, pos_s, wts, sample=True)
    ck = cache_k.reshape(n_phys * page * 2 * DIFF_HEADS, DIFF_DK)
    cv = (cache_v.reshape(n_phys, page, DIFF_HEADS, DIFF_DV // LANES, LANES)
          .transpose(0, 1, 3, 2, 4).reshape(n_phys * page * 2 * DIFF_HEADS, LANES))
    od_s = _attn_sample(dq_s, dk_s, dv_s, ck, cv, page_table, lams, g_sub, td, page)
    og_s, s_s = _gla(gqk_s, gv_s, la_s, gr_s, g_gl, state_gla[0], nbs, F32)
    h_s, hn_s = _outproj(od_s, og_s, xs, w_out_b, g_ff)
    buf = state_ffn_conv[0]
    zeros = jnp.zeros((nbs, td, D_FF), F32)
    p1 = zeros.at[:, 0].set(buf[:, 1]).reshape(ns, D_FF)
    p2 = zeros.at[:, 0].set(buf[:, 0]).at[:, 1].set(buf[:, 1]).reshape(ns, D_FF)
    y_s, gate_s = _ffn(hn_s, h_s, w_up_b, w_conv0, b_conv0, w_down_b, g_fin, p1, p2, seq=td)
    conv_s = jnp.concatenate([buf, gate_s.reshape(nbs, td, D_FF)], axis=1)[:, td:]

    def v_out(rows, lead):
        a = rows.reshape(-1, DIFF_DV // LANES, DIFF_HEADS, LANES).transpose(0, 2, 1, 3)
        return a.reshape(*lead, DIFF_HEADS, DIFF_DV)

    return (y_p.reshape(1, t, D_MODEL), y_s.reshape(nbs, td, D_MODEL),
            dk.reshape(1, 1, t, DIFF_HEADS, 2, DIFF_DK), v_out(dv, (1, 1, t)),
            s_p.reshape(1, 1, GLA_HEADS, GLA_DK, GLA_DV), conv_p.reshape(1, 1, CONV_W - 1, D_FF),
            dk_s.reshape(1, nbs, td, DIFF_HEADS, 2, DIFF_DK), v_out(dv_s, (1, nbs, td)),
            s_s.reshape(1, nbs, GLA_HEADS, GLA_DK, GLA_DV), conv_s.reshape(1, nbs, CONV_W - 1, D_FF))
```

```python
import functools
import math

import jax
import jax.numpy as jnp
from jax import lax
from jax.experimental import pallas as pl
from jax.experimental.pallas import tpu as pltpu

F32 = jnp.float32
BF16 = jnp.bfloat16

D_MODEL = 2048
DIFF_HEADS = 4
DIFF_DK = 128
DIFF_DV = 256
GLA_HEADS = 4
GLA_DK = 128
GLA_DV = 256
GLA_RANK = 16
GLA_TAU = 16.0
GLA_CHUNK = 64
ROPE_THETA = 500000.0
ROT_DIM = DIFF_DK // 4
ROT_HALF = ROT_DIM // 2
D_FF = 5632
CONV_W = 3
EPS = 1e-6
LAM_INIT = 0.8 - 0.6 * math.exp(-0.3 * 0)
LOG2_E = math.log2(math.e)

DIFF_QK = DIFF_HEADS * 2 * DIFF_DK
DIFF_V = DIFF_HEADS * DIFF_DV
GLA_QK = GLA_HEADS * GLA_DK
GLA_V = GLA_HEADS * GLA_DV
LANES = 128
SUBLANES = 8

VMEM_LIMIT = 56 * 1024 * 1024

PAGES_PER_STEP = 16
NEG_INF = float("-inf")


def _cparams(sem):
    return pltpu.CompilerParams(dimension_semantics=sem, vmem_limit_bytes=VMEM_LIMIT)


def _tile(n, pref):
    t = min(n, pref)
    assert n % t == 0, (n, pref)
    return t


def _rms(x, g):
    return x * lax.rsqrt(jnp.mean(x * x, axis=-1, keepdims=True) + EPS) * g


def _silu(x):
    return x * (1.0 / (1.0 + jnp.exp(-x)))


def _dot(a, b):
    return jnp.dot(a, b, preferred_element_type=F32)


def _dot_nt(a, b):
    return lax.dot_general(a, b, (((1,), (1,)), ((), ())), preferred_element_type=F32)


def _dot_tn(a, b):
    return lax.dot_general(a, b, (((0,), (0,)), ((), ())), preferred_element_type=F32)


def _norm_kernel(x_ref, g_ref, pos_ref, freq_ref, xn_ref, cos_ref, sin_ref):
    xn_ref[...] = _rms(x_ref[...], g_ref[...]).astype(xn_ref.dtype)
    ang = pos_ref[...] * freq_ref[...]
    lane = lax.broadcasted_iota(jnp.int32, ang.shape, 1)
    s = jnp.sin(ang)
    cos_ref[...] = jnp.cos(ang)
    sin_ref[...] = jnp.where(lane < ROT_HALF, -s, s)


def _norm(x, g, pos, freq_lane):
    n = x.shape[0]
    tm = _tile(n, 512)
    return pl.pallas_call(
        _norm_kernel,
        grid=(n // tm,),
        in_specs=[pl.BlockSpec((tm, D_MODEL), lambda i: (i, 0)),
                  pl.BlockSpec((1, D_MODEL), lambda i: (0, 0)),
                  pl.BlockSpec((tm, 1), lambda i: (i, 0)),
                  pl.BlockSpec((1, LANES), lambda i: (0, 0))],
        out_specs=[pl.BlockSpec((tm, D_MODEL), lambda i: (i, 0)),
                   pl.BlockSpec((tm, LANES), lambda i: (i, 0)),
                   pl.BlockSpec((tm, LANES), lambda i: (i, 0))],
        out_shape=[jax.ShapeDtypeStruct((n, D_MODEL), BF16),
                   jax.ShapeDtypeStruct((n, LANES), F32),
                   jax.ShapeDtypeStruct((n, LANES), F32)],
        compiler_params=_cparams(("parallel",)),
        name="norm",
    )(x, g, pos, freq_lane)


def _rope(y, cos, sin):
    lane = lax.broadcasted_iota(jnp.int32, cos.shape, 1)
    outs = []
    for s in range(y.shape[1] // LANES):
        x = y[:, s * LANES:(s + 1) * LANES]
        partner = jnp.where(lane < ROT_HALF,
                            pltpu.roll(x, LANES - ROT_HALF, 1),
                            pltpu.roll(x, ROT_HALF, 1))
        outs.append(x * cos + partner * sin)
    return jnp.concatenate(outs, axis=1)


def _store_cache_rows(o_ref, y, r0, kind):
    rb = y.shape[0]
    n_rows = DIFF_QK // LANES
    for s in range(n_rows):
        if kind == "rope_k":
            dst = s
        else:
            h, half = divmod(s, DIFF_DV // LANES)
            dst = half * DIFF_HEADS + h
        o_ref[pl.ds(r0 * n_rows + dst, rb, stride=n_rows), :] = y[:, s * LANES:(s + 1) * LANES]


def _proj_kernel(*refs, kind, n_std, rb):
    wb_ref = refs[-1]
    refs = refs[:-1]
    if kind in ("rope_q", "rope_k"):
        xn_ref, w_ref, cos_ref, sin_ref = refs[:4]
        outs = refs[4:]
    elif kind == "gate":
        xn_ref, w_ref, w2_ref, b_ref = refs[:4]
        outs = refs[4:]
    else:
        xn_ref, w_ref = refs[:2]
        outs = refs[2:]

    @pl.when(pl.program_id(0) == 0)
    def _():
        wb_ref[...] = w_ref[...].T.astype(BF16)

    for r0 in range(0, xn_ref.shape[0], rb):
        rows = slice(r0, r0 + rb)
        y = _dot(xn_ref[rows, :], wb_ref[...])
        if kind == "rope_q":
            y = _rope(y, cos_ref[rows, :], sin_ref[rows, :]) * (DIFF_DK ** -0.5 * LOG2_E)
        elif kind == "rope_k":
            y = _rope(y, cos_ref[rows, :], sin_ref[rows, :])
        elif kind == "gla_qk":
            col = lax.broadcasted_iota(jnp.int32, y.shape, 1)
            y = jnp.where(col < GLA_QK, y * (GLA_DK ** -0.5), y)
        elif kind == "gate":
            z = _dot(y.astype(BF16), w2_ref[...].astype(BF16)) + b_ref[...]
            y = (jnp.minimum(z, 0.0) - jnp.log1p(jnp.exp(-jnp.abs(z)))) / GLA_TAU
        for o in outs[:n_std]:
            o[rows, :] = y.astype(o.dtype)
        for o in outs[n_std:]:
            _store_cache_rows(o, y, r0, kind)


def _proj(xn, wt, col_block, width, kind, out_dtypes, extra=(), cache_rows=False):
    n = xn.shape[0]
    tm = _tile(n, 1024)
    tn_out = GLA_QK if kind == "gate" else width
    extra_specs = []
    for e in extra:
        if e.shape[0] == n:
            extra_specs.append(pl.BlockSpec((tm, e.shape[1]), lambda i: (i, 0)))
        else:
            extra_specs.append(pl.BlockSpec(e.shape, lambda i: (0, 0)))
    out_specs = [pl.BlockSpec((tm, tn_out), lambda i: (i, 0)) for _ in out_dtypes]
    out_shape = [jax.ShapeDtypeStruct((n, tn_out), dt) for dt in out_dtypes]
    if cache_rows:
        per_tok = width // LANES
        out_specs.append(pl.BlockSpec((tm * per_tok, LANES), lambda i: (i, 0)))
        out_shape.append(jax.ShapeDtypeStruct((n * per_tok, LANES), F32))
    return pl.pallas_call(
        functools.partial(_proj_kernel, kind=kind, n_std=len(out_dtypes), rb=_tile(tm, 256)),
        grid=(n // tm,),
        in_specs=[pl.BlockSpec((tm, D_MODEL), lambda i: (i, 0)),
                  pl.BlockSpec((width, D_MODEL), lambda i: (col_block, 0), pipeline_mode=pl.Buffered(1))]
                 + extra_specs,
        out_specs=out_specs,
        out_shape=out_shape,
        scratch_shapes=[pltpu.VMEM((D_MODEL, width), BF16)],
        compiler_params=_cparams(("arbitrary",)),
        name="proj_" + kind,
    )(xn, wt, *extra)


def _lam(lq1, lk1, lq2, lk2):
    return (jnp.exp(jnp.sum(lq1 * lk1, axis=-1, keepdims=True))
            - jnp.exp(jnp.sum(lq2 * lk2, axis=-1, keepdims=True)) + LAM_INIT)


def _attn_prompt_kernel(q_ref, k_ref, v_ref, lq1, lk1, lq2, lk2, g_ref, o_ref,
                        m_ref, l_ref, acc_ref, *, tq, rb):
    qi = pl.program_id(1)
    m_ref[...] = jnp.full(m_ref.shape, NEG_INF, F32)
    l_ref[...] = jnp.zeros(l_ref.shape, F32)
    acc_ref[...] = jnp.zeros(acc_ref.shape, F32)

    def block(kb, diagonal):
        r0 = pl.multiple_of(kb * tq, tq)
        k = k_ref[pl.ds(r0, tq), :]
        v = v_ref[pl.ds(r0, tq), :]
        for r in range(2 * tq // rb):
            amap, q0 = divmod(r * rb, tq)
            rows = slice(r * rb, (r + 1) * rb)
            dcols = slice(amap * DIFF_DK, (amap + 1) * DIFF_DK)
            nk = tq
            s = _dot_nt(q_ref[q0:q0 + rb, dcols], k[:nk, dcols])
            if diagonal:
                row = q0 + lax.broadcasted_iota(jnp.int32, s.shape, 0)
                col = lax.broadcasted_iota(jnp.int32, s.shape, 1)
                s = jnp.where(col <= row, s, NEG_INF)
            chunks = [s[:, c * LANES:(c + 1) * LANES] for c in range(nk // LANES)]
            mx = functools.reduce(jnp.maximum, chunks)
            m_old = m_ref[rows]
            m_new = jnp.maximum(m_old, jnp.max(mx, axis=-1, keepdims=True))
            alpha = jnp.exp2(m_old - m_new)
            ps = [jnp.exp2(c - m_new) for c in chunks]
            l_ref[rows] = alpha * l_ref[rows] + functools.reduce(jnp.add, ps)
            pv = _dot(jnp.concatenate(ps, axis=1).astype(BF16), v[:nk])
            acc_ref[rows] = jnp.concatenate([alpha] * (DIFF_DV // LANES), axis=1) * acc_ref[rows] + pv
            m_ref[rows] = m_new

    def body(kb, carry):
        block(kb, False)
        return carry

    lax.fori_loop(0, qi, body, 0)
    block(qi, True)

    o = acc_ref[...] / jnp.sum(l_ref[...], axis=-1, keepdims=True)
    lam = _lam(lq1[...], lk1[...], lq2[...], lk2[...])
    o = o[:tq] - lam * o[tq:]
    o_ref[...] = (_rms(o, g_ref[...]) * (1.0 - LAM_INIT)).astype(o_ref.dtype)


def _attn_prompt(q, k, v, lams, g_subln):
    t = q.shape[0]
    tq = _tile(t, 1024)
    rb = _tile(tq, 128)
    vec = pl.BlockSpec((1, DIFF_DK), lambda h, i: (0, 0))
    return pl.pallas_call(
        functools.partial(_attn_prompt_kernel, tq=tq, rb=rb),
        grid=(DIFF_HEADS, t // tq),
        in_specs=[pl.BlockSpec((tq, 2 * DIFF_DK), lambda h, i: (i, h)),
                  pl.BlockSpec((t, 2 * DIFF_DK), lambda h, i: (0, h)),
                  pl.BlockSpec((t, DIFF_DV), lambda h, i: (0, h)),
                  vec, vec, vec, vec,
                  pl.BlockSpec((1, DIFF_DV), lambda h, i: (0, 0))],
        out_specs=pl.BlockSpec((tq, DIFF_DV), lambda h, i: (i, h)),
        out_shape=jax.ShapeDtypeStruct((t, DIFF_V), BF16),
        scratch_shapes=[pltpu.VMEM((2 * tq, LANES), F32), pltpu.VMEM((2 * tq, LANES), F32),
                        pltpu.VMEM((2 * tq, DIFF_DV), F32)],
        compiler_params=_cparams(("parallel", "parallel")),
        name="attn_prompt",
    )(q, k, v, *lams, g_subln)


def _attn_sample_kernel(pt_ref, q_ref, kn_ref, vn_ref, *rest, td, pages, page):
    del pt_ref
    k_pages = rest[:pages]
    v_pages = rest[pages:2 * pages]
    lq1, lk1, lq2, lk2, g_ref, o_ref, m_ref, l_ref, acc_ref = rest[2 * pages:]
    c = pl.program_id(1)
    n_hm = 2 * DIFF_HEADS
    halves = DIFF_DV // LANES

    @pl.when(c == 0)
    def _():
        m_ref[...] = jnp.full(m_ref.shape, NEG_INF, F32)
        l_ref[...] = jnp.zeros(l_ref.shape, F32)
        acc_ref[...] = jnp.zeros(acc_ref.shape, F32)

    q = q_ref[...].astype(BF16)

    def q_hm(hm):
        return q[:, hm * DIFF_DK:(hm + 1) * DIFF_DK]

    def k_rows(refs, n_tok, hm):
        return jnp.concatenate([r[pl.ds(hm, n_tok, stride=n_hm), :].astype(BF16) for r in refs], axis=0)

    def v_rows(refs, n_tok, h):
        return jnp.concatenate(
            [jnp.concatenate([r[pl.ds(half * DIFF_HEADS + h, n_tok, stride=halves * DIFF_HEADS), :].astype(BF16)
                              for half in range(halves)], axis=1) for r in refs], axis=0)

    def update(s, v_of_head):
        m_old = m_ref[...]
        m_new = jnp.maximum(m_old, jnp.max(s, axis=-1, keepdims=True))
        alpha = jnp.exp2(m_old - m_new)
        p = jnp.exp2(s - m_new)
        l_ref[...] = alpha * l_ref[...] + jnp.sum(p, axis=-1, keepdims=True)
        pb = p.astype(BF16)
        pv = jnp.concatenate([_dot(pb[2 * h * td:2 * (h + 1) * td], v_of_head(h))
                              for h in range(DIFF_HEADS)], axis=0)
        acc_ref[...] = alpha * acc_ref[...] + pv
        m_ref[...] = m_new

    update(jnp.concatenate([_dot_nt(q_hm(hm), k_rows(k_pages, page, hm)) for hm in range(n_hm)], axis=0),
           functools.partial(v_rows, v_pages, page))

    @pl.when(c == pl.num_programs(1) - 1)
    def _():
        s = jnp.concatenate([_dot_nt(q_hm(hm), k_rows([kn_ref], td, hm)) for hm in range(n_hm)], axis=0)
        t_q = lax.broadcasted_iota(jnp.int32, s.shape, 0) % td
        t_k = lax.broadcasted_iota(jnp.int32, s.shape, 1)
        update(jnp.where(t_k <= t_q, s, NEG_INF), functools.partial(v_rows, [vn_ref], td))
        o = acc_ref[...] / l_ref[...]
        lam = _lam(lq1[...], lk1[...], lq2[...], lk2[...])
        for h in range(DIFF_HEADS):
            r1 = 2 * h * td
            oh = o[r1:r1 + td] - lam * o[r1 + td:r1 + 2 * td]
            o_ref[:, h * DIFF_DV:(h + 1) * DIFF_DV] = (_rms(oh, g_ref[...]) * (1.0 - LAM_INIT)).astype(o_ref.dtype)


def _attn_sample(q, k_new, v_new, cache_k, cache_v, page_table, lams, g_subln, td, page):
    nb, n_pages = page_table.shape
    pages = PAGES_PER_STEP
    assert n_pages % pages == 0
    rows = 2 * DIFF_HEADS * td
    page_rows = page * DIFF_QK // LANES

    def tok(b, c, pt):
        return (b, 0)

    def page_spec(i):
        return pl.BlockSpec((page_rows, LANES), lambda b, c, pt: (pt[b * n_pages + c * pages + i], 0))

    const = lambda shape: pl.BlockSpec(shape, lambda b, c, pt: (0, 0))
    grid_spec = pltpu.PrefetchScalarGridSpec(
        num_scalar_prefetch=1,
        grid=(nb, n_pages // pages),
        in_specs=[pl.BlockSpec((td, DIFF_QK), tok), pl.BlockSpec((td * DIFF_QK // LANES, LANES), tok),
                  pl.BlockSpec((td * DIFF_V // LANES, LANES), tok)]
                 + [page_spec(i) for i in range(pages)] + [page_spec(i) for i in range(pages)]
                 + [const((1, DIFF_DK))] * 4 + [const((1, DIFF_DV))],
        out_specs=pl.BlockSpec((td, DIFF_V), tok),
        scratch_shapes=[pltpu.VMEM((rows, 1), F32), pltpu.VMEM((rows, 1), F32),
                        pltpu.VMEM((rows, DIFF_DV), F32)],
    )
    return pl.pallas_call(
        functools.partial(_attn_sample_kernel, td=td, pages=pages, page=page),
        grid_spec=grid_spec,
        out_shape=jax.ShapeDtypeStruct((nb * td, DIFF_V), F32),
        compiler_params=_cparams(("parallel", "arbitrary")),
        name="attn_sample",
    )(page_table.reshape(-1), q, k_new, v_new, *([cache_k] * pages), *([cache_v] * pages), *lams, g_subln)


def _gla_level_matrices(c):
    levels = c.bit_length() - 1
    t = lax.broadcasted_iota(jnp.int32, (c, c), 0)
    s = lax.broadcasted_iota(jnp.int32, (c, c), 1)
    tril = (s <= t).astype(F32)
    blocks, masks = [tril], []
    for l in range(1, levels + 1):
        grp, half = 1 << l, 1 << (l - 1)
        rho = (t // grp) * grp + half - 1
        blocks.append(tril - (s <= rho).astype(F32))
        masks.append(((t // grp) == (s // grp)) & ((t % grp) >= half) & ((s % grp) < half))
    return jnp.concatenate(blocks, axis=0), masks, (s == t)


def _split3_dot(m3, g):
    g1 = g.astype(BF16)
    r1 = g - g1.astype(F32)
    g2 = r1.astype(BF16)
    g3 = (r1 - g2.astype(F32)).astype(BF16)
    return _dot(m3, jnp.concatenate([g1, g2, g3], axis=0))


def _gla_kernel(qk_ref, v_ref, la_ref, gr_ref, g_ref, s0_ref, o_ref, sfin_ref, s_ref, *, chunk, n_chunks):
    @pl.when(pl.program_id(1) == 0)
    def _():
        s_ref[...] = s0_ref[...]

    mstack, masks, eye = _gla_level_matrices(chunk)
    m3 = jnp.concatenate([mstack.astype(BF16)] * 3, axis=1)
    eye_dk = (lax.broadcasted_iota(jnp.int32, (GLA_DK, GLA_DK), 0)
              == lax.broadcasted_iota(jnp.int32, (GLA_DK, GLA_DK), 1))

    def do_chunk(ci, carry):
        r0 = pl.multiple_of(ci * chunk, chunk)
        rows = pl.ds(r0, chunk)
        ex_all = _split3_dot(m3, la_ref[rows, :])
        for h in range(GLA_HEADS):
            kcols = slice(h * GLA_DK, (h + 1) * GLA_DK)
            vcols = slice(h * GLA_DV, (h + 1) * GLA_DV)
            q = qk_ref[rows, kcols]
            k = qk_ref[rows, GLA_QK + h * GLA_DK:GLA_QK + (h + 1) * GLA_DK]
            v = v_ref[rows, vcols].astype(BF16)
            state = s_ref[h]

            ex = ex_all[:, kcols]
            b = ex[:chunk]
            blast = b[chunk - 1:chunk]
            o = _dot((q * jnp.exp(b)).astype(BF16), state.astype(BF16))
            att = jnp.where(eye, _dot_nt(q.astype(BF16), k.astype(BF16)), 0.0)
            for l, mask in enumerate(masks):
                d = ex[(l + 1) * chunk:(l + 2) * chunk]
                a_l = (q * jnp.exp(jnp.minimum(d, 0.0))).astype(BF16)
                b_l = (k * jnp.exp(jnp.minimum(-d, 0.0))).astype(BF16)
                att = att + jnp.where(mask, _dot_nt(a_l, b_l), 0.0)
            o = o + _dot(att.astype(BF16), v)

            kdec = (k * jnp.exp(blast - b)).astype(BF16)
            dec_col = jnp.sum(jnp.where(eye_dk, jnp.exp(blast), 0.0), axis=1, keepdims=True)
            s_ref[h] = dec_col * state + _dot_tn(kdec, v)

            gate = _silu(gr_ref[rows, vcols])
            o_ref[rows, vcols] = (_rms(o, g_ref[...]) * gate).astype(o_ref.dtype)
        return carry

    lax.fori_loop(0, n_chunks, do_chunk, 0)

    @pl.when(pl.program_id(1) == pl.num_programs(1) - 1)
    def _():
        sfin_ref[...] = s_ref[...]


def _gla(qk, v, la, gr, g_gla, s0, nb, out_dtype):
    n = qk.shape[0]
    t = n // nb
    chunk = min(GLA_CHUNK, t)
    assert t % chunk == 0 and chunk & (chunk - 1) == 0
    tt = _tile(t, 512)
    n_t = t // tt
    row = lambda b, i: (b * n_t + i, 0)
    st = lambda b, i: (b, 0, 0, 0)
    return pl.pallas_call(
        functools.partial(_gla_kernel, chunk=chunk, n_chunks=tt // chunk),
        grid=(nb, n_t),
        in_specs=[pl.BlockSpec((tt, 2 * GLA_QK), row), pl.BlockSpec((tt, GLA_V), row),
                  pl.BlockSpec((tt, GLA_QK), row), pl.BlockSpec((tt, GLA_V), row),
                  pl.BlockSpec((1, GLA_DV), lambda b, i: (0, 0)),
                  pl.BlockSpec((None, GLA_HEADS, GLA_DK, GLA_DV), st)],
        out_specs=[pl.BlockSpec((tt, GLA_V), row),
                   pl.BlockSpec((None, GLA_HEADS, GLA_DK, GLA_DV), st)],
        out_shape=[jax.ShapeDtypeStruct((n, GLA_V), out_dtype),
                   jax.ShapeDtypeStruct((nb, GLA_HEADS, GLA_DK, GLA_DV), F32)],
        scratch_shapes=[pltpu.VMEM((GLA_HEADS, GLA_DK, GLA_DV), F32)],
        compiler_params=_cparams(("parallel", "arbitrary")),
        name="gla",
    )(qk, v, la, gr, g_gla, s0)


def _outproj_kernel(od_ref, og_ref, x_ref, w_ref, g_ref, h_ref, hn_ref):
    h = (x_ref[...] + _dot(od_ref[...].astype(BF16), w_ref[:DIFF_V, :])
         + _dot(og_ref[...].astype(BF16), w_ref[DIFF_V:, :]))
    h_ref[...] = h
    hn_ref[...] = _rms(h, g_ref[...]).astype(hn_ref.dtype)


def _outproj(od, og, x, w_out, g_ffn):
    n = x.shape[0]
    tm = _tile(n, 512)
    return pl.pallas_call(
        _outproj_kernel,
        grid=(n // tm,),
        in_specs=[pl.BlockSpec((tm, DIFF_V), lambda i: (i, 0)), pl.BlockSpec((tm, GLA_V), lambda i: (i, 0)),
                  pl.BlockSpec((tm, D_MODEL), lambda i: (i, 0)),
                  pl.BlockSpec((D_MODEL, D_MODEL), lambda i: (0, 0)),
                  pl.BlockSpec((1, D_MODEL), lambda i: (0, 0))],
        out_specs=[pl.BlockSpec((tm, D_MODEL), lambda i: (i, 0)), pl.BlockSpec((tm, D_MODEL), lambda i: (i, 0))],
        out_shape=[jax.ShapeDtypeStruct((n, D_MODEL), F32), jax.ShapeDtypeStruct((n, D_MODEL), BF16)],
        compiler_params=_cparams(("parallel",)),
        name="outproj",
    )(od, og, x, w_out, g_ffn)


def _ffn_kernel(hn_ref, h_ref, wg_ref, wv_ref, wc_ref, bc_ref, wd_ref, gf_ref, p1_ref, p2_ref,
                y_ref, gate_ref, carry_ref, *, seq, tf, rb):
    i, j = pl.program_id(0), pl.program_id(1)
    tm = hn_ref.shape[0]

    @pl.when(j == 0)
    def _():
        y_ref[...] = h_ref[...]

    if not seq:
        cols = pl.ds(pl.multiple_of(j * tf, tf), tf)

        @pl.when(i == 0)
        def _():
            carry_ref[:, cols] = p1_ref[...]
        prev = carry_ref[:, cols]
    wc = wc_ref[...]
    for r in range(tm // rb):
        rows = slice(r * rb, (r + 1) * rb)
        hn = hn_ref[rows, :]
        gate = _dot(hn, wg_ref[...])
        val = _dot(hn, wv_ref[...])
        row = lax.broadcasted_iota(jnp.int32, gate.shape, 0)
        g1 = pltpu.roll(gate, 1, 0)
        g2 = pltpu.roll(gate, 2, 0)
        if seq:
            t = row % seq
            g1 = jnp.where(t == 0, p1_ref[rows, :], g1)
            g2 = jnp.where(t < 2, p2_ref[rows, :], g2)
            gate_ref[rows, :] = gate
        else:
            last, last2 = prev[SUBLANES - 1:SUBLANES], prev[SUBLANES - 2:SUBLANES - 1]
            g1 = jnp.where(row == 0, last, g1)
            g2 = jnp.where(row == 0, last2, jnp.where(row == 1, last, g2))
            prev = gate[rb - SUBLANES:]
        conv = bc_ref[...] + wc[0:1] * g2 + wc[1:2] * g1 + wc[2:3] * gate
        u = (_silu(conv) * val).astype(BF16)
        y_ref[rows, :] += _dot(u, wd_ref[...])
    if not seq:
        carry_ref[:, cols] = prev
        gate_ref[...] = prev

    @pl.when(j == pl.num_programs(1) - 1)
    def _():
        y_ref[...] = _rms(y_ref[...], gf_ref[...])


def _ffn(hn, h, w_up, w_conv, b_conv, w_down, g_final, p1, p2, seq):
    n = hn.shape[0]
    tm = _tile(n, 1024)
    tf = 512
    assert D_FF % tf == 0
    n_j = D_FF // tf
    if seq:
        assert n == tm and tm % seq == 0
        prev_spec = pl.BlockSpec((tm, tf), lambda i, j: (0, j))
        gate_rows = tm
    else:
        prev_spec = pl.BlockSpec((SUBLANES, tf), lambda i, j: (0, j))
        gate_rows = SUBLANES
    return pl.pallas_call(
        functools.partial(_ffn_kernel, seq=seq, tf=tf, rb=_tile(tm, 256)),
        grid=(n // tm, n_j),
        in_specs=[pl.BlockSpec((tm, D_MODEL), lambda i, j: (i, 0)),
                  pl.BlockSpec((tm, D_MODEL), lambda i, j: (i, 0)),
                  pl.BlockSpec((D_MODEL, tf), lambda i, j: (0, j)),
                  pl.BlockSpec((D_MODEL, tf), lambda i, j: (0, n_j + j)),
                  pl.BlockSpec((CONV_W, tf), lambda i, j: (0, j)),
                  pl.BlockSpec((1, tf), lambda i, j: (0, j)),
                  pl.BlockSpec((tf, D_MODEL), lambda i, j: (j, 0)),
                  pl.BlockSpec((1, D_MODEL), lambda i, j: (0, 0)),
                  prev_spec, prev_spec],
        out_specs=[pl.BlockSpec((tm, D_MODEL), lambda i, j: (i, 0)),
                   pl.BlockSpec((gate_rows, tf), lambda i, j: (i, j))],
        out_shape=[jax.ShapeDtypeStruct((n, D_MODEL), F32),
                   jax.ShapeDtypeStruct((n // tm * gate_rows, D_FF), F32)],
        scratch_shapes=[pltpu.VMEM((SUBLANES, D_FF), F32)],
        compiler_params=_cparams(("arbitrary", "arbitrary")),
        name="ffn",
    )(hn, h, w_up, w_up, w_conv, b_conv, w_down, g_final, p1, p2)


def _mixer_inputs(x, pos, wts, sample):
    xn, cos, sin = _norm(x, wts["g_mix"], pos, wts["freq_lane"])
    w = wts["w_in_t"]
    wide = DIFF_QK
    assert DIFF_V == wide and 2 * GLA_QK == wide and GLA_V == wide and wide % GLA_RANK == 0
    act = F32 if sample else BF16
    kv_std = () if sample else (BF16,)
    (dq,) = _proj(xn, w, 0, wide, "rope_q", (act,), (cos, sin))
    dk = _proj(xn, w, 1, wide, "rope_k", kv_std, (cos, sin), cache_rows=True)
    dv = _proj(xn, w, 2, wide, "plain", kv_std, cache_rows=True)
    (gqk,) = _proj(xn, w, 3, wide, "gla_qk", (F32,))
    (gv,) = _proj(xn, w, 4, wide, "plain", (act,))
    (gr,) = _proj(xn, w, 5, wide, "plain", (F32,))
    (la,) = _proj(xn, w, 6 * wide // GLA_RANK, GLA_RANK, "gate", (F32,), (wts["w_g2"], wts["b_g"]))
    return dq, dk, dv, gqk, gv, gr, la


def kernel(x_prompt, x_sample, cache_k, cache_v, state_gla, state_ffn_conv, page_table, g_mix, w_in, lam_q1, lam_k1, lam_q2, lam_k2, g_subln, w_g2, b_g, g_gla, w_out, g_ffn, w_up, w_conv, b_conv, w_down, g_final):
    nbp, t, _ = x_prompt.shape
    nbs, td, _ = x_sample.shape
    assert nbp == 1 and g_mix.shape[0] == 1
    n_phys, page = cache_k.shape[1], cache_k.shape[2]
    past_len = page_table.shape[1] * page

    inv_freq = ROPE_THETA ** (-jnp.arange(ROT_HALF, dtype=F32) * (2.0 / ROT_DIM))
    freq_lane = jnp.concatenate([inv_freq, inv_freq, jnp.zeros((LANES - ROT_DIM,), F32)]).reshape(1, LANES)
    row2 = lambda a: a[0].reshape(1, -1)
    wts = dict(g_mix=row2(g_mix), freq_lane=freq_lane, w_in_t=w_in[0].T, w_g2=w_g2[0], b_g=row2(b_g))
    lams = tuple(row2(a) for a in (lam_q1, lam_k1, lam_q2, lam_k2))
    g_sub, g_gl, g_ff, g_fin = row2(g_subln), row2(g_gla), row2(g_ffn), g_final.reshape(1, D_MODEL)
    w_out_b, w_up_b, w_down_b = w_out[0].astype(BF16), w_up[0].astype(BF16), w_down[0].astype(BF16)
    w_conv0, b_conv0 = w_conv[0], row2(b_conv)

    xp = x_prompt.reshape(t, D_MODEL)
    pos_p = jnp.arange(t, dtype=jnp.int32).astype(F32).reshape(t, 1)
    dq, (dk_b, dk), (dv_b, dv), gqk, gv, gr, la = _mixer_inputs(xp, pos_p, wts, sample=False)
    od = _attn_prompt(dq, dk_b, dv_b, lams, g_sub)
    og, s_p = _gla(gqk, gv, la, gr, g_gl, jnp.zeros((1, GLA_HEADS, GLA_DK, GLA_DV), F32), 1, BF16)
    h, hn = _outproj(od, og, xp, w_out_b, g_ff)
    buf0 = jnp.zeros((SUBLANES, D_FF), F32)
    y_p, tail = _ffn(hn, h, w_up_b, w_conv0, b_conv0, w_down_b, g_fin, buf0, buf0, seq=0)
    conv_p = tail[tail.shape[0] - (CONV_W - 1):]

    ns = nbs * td
    xs = x_sample.reshape(ns, D_MODEL)
    pos_s = jnp.tile(past_len + jnp.arange(td, dtype=jnp.int32), nbs).astype(F32).reshape(ns, 1)
    dq_s, (dk_s,), (dv_s,), gqk_s, gv_s, gr_s, la_s = _mixer_inputs(xs, pos_s, wts, sample=True)
    ck = cache_k.reshape(n_phys * page * 2 * DIFF_HEADS, DIFF_DK)
    cv = (cache_v.reshape(n_phys, page, DIFF_HEADS, DIFF_DV // LANES, LANES)
          .transpose(0, 1, 3, 2, 4).reshape(n_phys * page * 2 * DIFF_HEADS, LANES))
    od_s = _attn_sample(dq_s, dk_s, dv_s, ck, cv, page_table, lams, g_sub, td, page)
    og_s, s_s = _gla(gqk_s, gv_s, la_s, gr_s, g_gl, state_gla[0], nbs, F32)
    h_s, hn_s = _outproj(od_s, og_s, xs, w_out_b, g_ff)
    buf = state_ffn_conv[0]
    zeros = jnp.zeros((nbs, td, D_FF), F32)
    p1 = zeros.at[:, 0].set(buf[:, 1]).reshape(ns, D_FF)
    p2 = zeros.at[:, 0].set(buf[:, 0]).at[:, 1].set(buf[:, 1]).reshape(ns, D_FF)
    y_s, gate_s = _ffn(hn_s, h_s, w_up_b, w_conv0, b_conv0, w_down_b, g_fin, p1, p2, seq=td)
    conv_s = jnp.concatenate([buf, gate_s.reshape(nbs, td, D_FF)], axis=1)[:, td:]

    def v_out(rows, lead):
        a = rows.reshape(-1, DIFF_DV // LANES, DIFF_HEADS, LANES).transpose(0, 2, 1, 3)
        return a.reshape(*lead, DIFF_HEADS, DIFF_DV)

    return (y_p.reshape(1, t, D_MODEL), y_s.reshape(nbs, td, D_MODEL),
            dk.reshape(1, 1, t, DIFF_HEADS, 2, DIFF_DK), v_out(dv, (1, 1, t)),
            s_p.reshape(1, 1, GLA_HEADS, GLA_DK, GLA_DV), conv_p.reshape(1, 1, CONV_W - 1, D_FF),
            dk_s.reshape(1, nbs, td, DIFF_HEADS, 2, DIFF_DK), v_out(dv_s, (1, nbs, td)),
            s_s.reshape(1, nbs, GLA_HEADS, GLA_DK, GLA_DV), conv_s.reshape(1, nbs, CONV_W - 1, D_FF))
```

```python
import functools
import math

import jax
import jax.numpy as jnp
from jax import lax
from jax.experimental import pallas as pl
from jax.experimental.pallas import tpu as pltpu

F32 = jnp.float32
BF16 = jnp.bfloat16

D_MODEL = 2048
DIFF_HEADS = 4
DIFF_DK = 128
DIFF_DV = 256
GLA_HEADS = 4
GLA_DK = 128
GLA_DV = 256
GLA_RANK = 16
GLA_TAU = 16.0
GLA_CHUNK = 128
ROPE_THETA = 500000.0
ROT_DIM = DIFF_DK // 4
ROT_HALF = ROT_DIM // 2
D_FF = 5632
CONV_W = 3
EPS = 1e-6
LAM_INIT = 0.8 - 0.6 * math.exp(-0.3 * 0)
LOG2_E = math.log2(math.e)

DIFF_QK = DIFF_HEADS * 2 * DIFF_DK
DIFF_V = DIFF_HEADS * DIFF_DV
GLA_QK = GLA_HEADS * GLA_DK
GLA_V = GLA_HEADS * GLA_DV
LANES = 128
SUBLANES = 8

VMEM_LIMIT = 56 * 1024 * 1024

PAGES_PER_STEP = 16
NEG_INF = float("-inf")


def _cparams(sem):
    return pltpu.CompilerParams(dimension_semantics=sem, vmem_limit_bytes=VMEM_LIMIT)


def _tile(n, pref):
    t = min(n, pref)
    assert n % t == 0, (n, pref)
    return t


def _rms(x, g):
    return x * lax.rsqrt(jnp.mean(x * x, axis=-1, keepdims=True) + EPS) * g


def _silu(x):
    return x * (1.0 / (1.0 + jnp.exp(-x)))


def _dot(a, b):
    return jnp.dot(a, b, preferred_element_type=F32)


def _dot_nt(a, b):
    return lax.dot_general(a, b, (((1,), (1,)), ((), ())), preferred_element_type=F32)


def _dot_tn(a, b):
    return lax.dot_general(a, b, (((0,), (0,)), ((), ())), preferred_element_type=F32)


def _norm_kernel(x_ref, g_ref, pos_ref, freq_ref, xn_ref, cos_ref, sin_ref):
    xn_ref[...] = _rms(x_ref[...], g_ref[...]).astype(xn_ref.dtype)
    ang = pos_ref[...] * freq_ref[...]
    lane = lax.broadcasted_iota(jnp.int32, ang.shape, 1)
    s = jnp.sin(ang)
    cos_ref[...] = jnp.cos(ang)
    sin_ref[...] = jnp.where(lane < ROT_HALF, -s, s)


def _norm(x, g, pos, freq_lane):
    n = x.shape[0]
    tm = _tile(n, 512)
    return pl.pallas_call(
        _norm_kernel,
        grid=(n // tm,),
        in_specs=[pl.BlockSpec((tm, D_MODEL), lambda i: (i, 0)),
                  pl.BlockSpec((1, D_MODEL), lambda i: (0, 0)),
                  pl.BlockSpec((tm, 1), lambda i: (i, 0)),
                  pl.BlockSpec((1, LANES), lambda i: (0, 0))],
        out_specs=[pl.BlockSpec((tm, D_MODEL), lambda i: (i, 0)),
                   pl.BlockSpec((tm, LANES), lambda i: (i, 0)),
                   pl.BlockSpec((tm, LANES), lambda i: (i, 0))],
        out_shape=[jax.ShapeDtypeStruct((n, D_MODEL), BF16),
                   jax.ShapeDtypeStruct((n, LANES), F32),
                   jax.ShapeDtypeStruct((n, LANES), F32)],
        compiler_params=_cparams(("parallel",)),
        name="norm",
    )(x, g, pos, freq_lane)


def _rope(y, cos, sin):
    lane = lax.broadcasted_iota(jnp.int32, cos.shape, 1)
    outs = []
    for s in range(y.shape[1] // LANES):
        x = y[:, s * LANES:(s + 1) * LANES]
        partner = jnp.where(lane < ROT_HALF,
                            pltpu.roll(x, LANES - ROT_HALF, 1),
                            pltpu.roll(x, ROT_HALF, 1))
        outs.append(x * cos + partner * sin)
    return jnp.concatenate(outs, axis=1)


def _store_cache_rows(o_ref, y, r0, kind):
    rb = y.shape[0]
    n_rows = DIFF_QK // LANES
    for s in range(n_rows):
        if kind == "rope_k":
            dst = s
        else:
            h, half = divmod(s, DIFF_DV // LANES)
            dst = half * DIFF_HEADS + h
        o_ref[pl.ds(r0 * n_rows + dst, rb, stride=n_rows), :] = y[:, s * LANES:(s + 1) * LANES]


def _proj_kernel(*refs, parts, n_extra, rb):
    wb_ref = refs[-1]
    xn_ref, w_ref = refs[:2]
    extras = refs[2:2 + n_extra]
    outs = refs[2 + n_extra:-1]
    pw = w_ref.shape[0] // len(parts)

    @pl.when(pl.program_id(0) == 0)
    def _():
        for p in range(len(parts)):
            wb_ref[:, p * pw:(p + 1) * pw] = w_ref[p * pw:(p + 1) * pw, :].T.astype(BF16)

    for r0 in range(0, xn_ref.shape[0], rb):
        rows = slice(r0, r0 + rb)
        y_all = _dot(xn_ref[rows, :], wb_ref[...])
        o = 0
        for p, (kind, n_std, cache_rows) in enumerate(parts):
            y = y_all[:, p * pw:(p + 1) * pw]
            if kind == "rope_q":
                y = _rope(y, extras[0][rows, :], extras[1][rows, :]) * (DIFF_DK ** -0.5 * LOG2_E)
            elif kind == "rope_k":
                y = _rope(y, extras[0][rows, :], extras[1][rows, :])
            elif kind == "gla_qk":
                col = lax.broadcasted_iota(jnp.int32, y.shape, 1)
                y = jnp.where(col < GLA_QK, y * (GLA_DK ** -0.5), y)
            elif kind == "gate":
                z = _dot(y.astype(BF16), extras[0][...].astype(BF16)) + extras[1][...]
                y = (jnp.minimum(z, 0.0) - jnp.log1p(jnp.exp(-jnp.abs(z)))) / GLA_TAU
            for oref in outs[o:o + n_std]:
                oref[rows, :] = y.astype(oref.dtype)
            o += n_std
            if cache_rows:
                _store_cache_rows(outs[o], y, r0, kind)
                o += 1


def _proj(xn, wt, first_row, parts, extra=()):
    n = xn.shape[0]
    tm = _tile(n, 1024)
    width = GLA_RANK if parts[0][0] == "gate" else DIFF_QK
    total = width * len(parts)
    assert first_row % total == 0
    extra_specs = []
    for e in extra:
        if e.shape[0] == n:
            extra_specs.append(pl.BlockSpec((tm, e.shape[1]), lambda i: (i, 0)))
        else:
            extra_specs.append(pl.BlockSpec(e.shape, lambda i: (0, 0)))
    out_specs, out_shape, kparts = [], [], []
    for kind, out_dtypes, cache_rows in parts:
        tn_out = GLA_QK if kind == "gate" else width
        out_specs += [pl.BlockSpec((tm, tn_out), lambda i: (i, 0)) for _ in out_dtypes]
        out_shape += [jax.ShapeDtypeStruct((n, tn_out), dt) for dt in out_dtypes]
        if cache_rows:
            per_tok = width // LANES
            out_specs.append(pl.BlockSpec((tm * per_tok, LANES), lambda i: (i, 0)))
            out_shape.append(jax.ShapeDtypeStruct((n * per_tok, LANES), F32))
        kparts.append((kind, len(out_dtypes), cache_rows))
    return pl.pallas_call(
        functools.partial(_proj_kernel, parts=tuple(kparts), n_extra=len(extra), rb=_tile(tm, 256)),
        grid=(n // tm,),
        in_specs=[pl.BlockSpec((tm, D_MODEL), lambda i: (i, 0)),
                  pl.BlockSpec((total, D_MODEL), lambda i: (first_row // total, 0), pipeline_mode=pl.Buffered(1))]
                 + extra_specs,
        out_specs=out_specs,
        out_shape=out_shape,
        scratch_shapes=[pltpu.VMEM((D_MODEL, total), BF16)],
        compiler_params=_cparams(("arbitrary",)),
        name="proj_" + "_".join(k for k, _, _ in parts),
    )(xn, wt, *extra)


def _lam(lq1, lk1, lq2, lk2):
    return (jnp.exp(jnp.sum(lq1 * lk1, axis=-1, keepdims=True))
            - jnp.exp(jnp.sum(lq2 * lk2, axis=-1, keepdims=True)) + LAM_INIT)


def _attn_prompt_kernel(q_ref, k_ref, v_ref, lq1, lk1, lq2, lk2, g_ref, o_ref,
                        m_ref, l_ref, acc_ref, *, tq, rb):
    qi = pl.program_id(1)
    m_ref[...] = jnp.full(m_ref.shape, NEG_INF, F32)
    l_ref[...] = jnp.zeros(l_ref.shape, F32)
    acc_ref[...] = jnp.zeros(acc_ref.shape, F32)

    def block(kb, diagonal):
        r0 = pl.multiple_of(kb * tq, tq)
        k = k_ref[pl.ds(r0, tq), :]
        v = v_ref[pl.ds(r0, tq), :]
        for r in range(2 * tq // rb):
            amap, q0 = divmod(r * rb, tq)
            rows = slice(r * rb, (r + 1) * rb)
            dcols = slice(amap * DIFF_DK, (amap + 1) * DIFF_DK)
            nk = tq
            s = _dot_nt(q_ref[q0:q0 + rb, dcols], k[:nk, dcols])
            if diagonal:
                row = q0 + lax.broadcasted_iota(jnp.int32, s.shape, 0)
                col = lax.broadcasted_iota(jnp.int32, s.shape, 1)
                s = jnp.where(col <= row, s, NEG_INF)
            chunks = [s[:, c0:c0 + LANES] for c0 in range(0, nk, LANES)]
            mx = functools.reduce(jnp.maximum, chunks)
            m_old = m_ref[rows]
            m_new = jnp.maximum(m_old, jnp.max(mx, axis=-1, keepdims=True))
            alpha = jnp.exp2(m_old - m_new)
            ps = [jnp.exp2(c - m_new) for c in chunks]
            l_ref[rows] = alpha * l_ref[rows] + functools.reduce(jnp.add, ps)
            pv = _dot(jnp.concatenate(ps, axis=1).astype(BF16), v[:nk])
            acc_ref[rows] = jnp.concatenate([alpha] * (DIFF_DV // LANES), axis=1) * acc_ref[rows] + pv
            m_ref[rows] = m_new

    def body(kb, carry):
        block(kb, False)
        return carry

    lax.fori_loop(0, qi, body, 0)
    block(qi, True)

    o = acc_ref[...] / jnp.sum(l_ref[...], axis=-1, keepdims=True)
    lam = _lam(lq1[...], lk1[...], lq2[...], lk2[...])
    o = o[:tq] - lam * o[tq:]
    o_ref[...] = (_rms(o, g_ref[...]) * (1.0 - LAM_INIT)).astype(o_ref.dtype)


def _attn_prompt(q, k, v, lams, g_subln):
    t = q.shape[0]
    tq = _tile(t, 1024)
    rb = _tile(tq, 128)
    vec = pl.BlockSpec((1, DIFF_DK), lambda h, i: (0, 0))
    return pl.pallas_call(
        functools.partial(_attn_prompt_kernel, tq=tq, rb=rb),
        grid=(DIFF_HEADS, t // tq),
        in_specs=[pl.BlockSpec((tq, 2 * DIFF_DK), lambda h, i: (i, h)),
                  pl.BlockSpec((t, 2 * DIFF_DK), lambda h, i: (0, h)),
                  pl.BlockSpec((t, DIFF_DV), lambda h, i: (0, h)),
                  vec, vec, vec, vec,
                  pl.BlockSpec((1, DIFF_DV), lambda h, i: (0, 0))],
        out_specs=pl.BlockSpec((tq, DIFF_DV), lambda h, i: (i, h)),
        out_shape=jax.ShapeDtypeStruct((t, DIFF_V), BF16),
        scratch_shapes=[pltpu.VMEM((2 * tq, LANES), F32), pltpu.VMEM((2 * tq, LANES), F32),
                        pltpu.VMEM((2 * tq, DIFF_DV), F32)],
        compiler_params=_cparams(("parallel", "parallel")),
        name="attn_prompt",
    )(q, k, v, *lams, g_subln)


def _attn_sample_kernel(pt_ref, q_ref, kn_ref, vn_ref, *rest, td, pages, page):
    del pt_ref
    k_pages = rest[:pages]
    v_pages = rest[pages:2 * pages]
    lq1, lk1, lq2, lk2, g_ref, o_ref, m_ref, l_ref, acc_ref = rest[2 * pages:]
    c = pl.program_id(1)
    n_hm = 2 * DIFF_HEADS
    halves = DIFF_DV // LANES

    @pl.when(c == 0)
    def _():
        m_ref[...] = jnp.full(m_ref.shape, NEG_INF, F32)
        l_ref[...] = jnp.zeros(l_ref.shape, F32)
        acc_ref[...] = jnp.zeros(acc_ref.shape, F32)

    q = q_ref[...].astype(BF16)

    def q_hm(hm):
        return q[:, hm * DIFF_DK:(hm + 1) * DIFF_DK]

    def k_rows(refs, n_tok, hm):
        return jnp.concatenate([r[pl.ds(hm, n_tok, stride=n_hm), :].astype(BF16) for r in refs], axis=0)

    def v_rows(refs, n_tok, h):
        return jnp.concatenate(
            [jnp.concatenate([r[pl.ds(half * DIFF_HEADS + h, n_tok, stride=halves * DIFF_HEADS), :].astype(BF16)
                              for half in range(halves)], axis=1) for r in refs], axis=0)

    def update(s, v_of_head):
        m_old = m_ref[...]
        m_new = jnp.maximum(m_old, jnp.max(s, axis=-1, keepdims=True))
        alpha = jnp.exp2(m_old - m_new)
        p = jnp.exp2(s - m_new)
        l_ref[...] = alpha * l_ref[...] + jnp.sum(p, axis=-1, keepdims=True)
        pb = p.astype(BF16)
        pv = jnp.concatenate([_dot(pb[2 * h * td:2 * (h + 1) * td], v_of_head(h))
                              for h in range(DIFF_HEADS)], axis=0)
        acc_ref[...] = alpha * acc_ref[...] + pv
        m_ref[...] = m_new

    update(jnp.concatenate([_dot_nt(q_hm(hm), k_rows(k_pages, page, hm)) for hm in range(n_hm)], axis=0),
           functools.partial(v_rows, v_pages, page))

    @pl.when(c == pl.num_programs(1) - 1)
    def _():
        s = jnp.concatenate([_dot_nt(q_hm(hm), k_rows([kn_ref], td, hm)) for hm in range(n_hm)], axis=0)
        t_q = lax.broadcasted_iota(jnp.int32, s.shape, 0) % td
        t_k = lax.broadcasted_iota(jnp.int32, s.shape, 1)
        update(jnp.where(t_k <= t_q, s, NEG_INF), functools.partial(v_rows, [vn_ref], td))
        o = acc_ref[...] / l_ref[...]
        lam = _lam(lq1[...], lk1[...], lq2[...], lk2[...])
        for h in range(DIFF_HEADS):
            r1 = 2 * h * td
            oh = o[r1:r1 + td] - lam * o[r1 + td:r1 + 2 * td]
            o_ref[:, h * DIFF_DV:(h + 1) * DIFF_DV] = (_rms(oh, g_ref[...]) * (1.0 - LAM_INIT)).astype(o_ref.dtype)


def _attn_sample(q, k_new, v_new, cache_k, cache_v, page_table, lams, g_subln, td, page):
    nb, n_pages = page_table.shape
    pages = PAGES_PER_STEP
    assert n_pages % pages == 0
    rows = 2 * DIFF_HEADS * td
    page_rows = page * DIFF_QK // LANES

    def tok(b, c, pt):
        return (b, 0)

    def page_spec(i):
        return pl.BlockSpec((page_rows, LANES), lambda b, c, pt: (pt[b * n_pages + c * pages + i], 0))

    const = lambda shape: pl.BlockSpec(shape, lambda b, c, pt: (0, 0))
    grid_spec = pltpu.PrefetchScalarGridSpec(
        num_scalar_prefetch=1,
        grid=(nb, n_pages // pages),
        in_specs=[pl.BlockSpec((td, DIFF_QK), tok), pl.BlockSpec((td * DIFF_QK // LANES, LANES), tok),
                  pl.BlockSpec((td * DIFF_V // LANES, LANES), tok)]
                 + [page_spec(i) for i in range(pages)] + [page_spec(i) for i in range(pages)]
                 + [const((1, DIFF_DK))] * 4 + [const((1, DIFF_DV))],
        out_specs=pl.BlockSpec((td, DIFF_V), tok),
        scratch_shapes=[pltpu.VMEM((rows, 1), F32), pltpu.VMEM((rows, 1), F32),
                        pltpu.VMEM((rows, DIFF_DV), F32)],
    )
    return pl.pallas_call(
        functools.partial(_attn_sample_kernel, td=td, pages=pages, page=page),
        grid_spec=grid_spec,
        out_shape=jax.ShapeDtypeStruct((nb * td, DIFF_V), F32),
        compiler_params=_cparams(("parallel", "arbitrary")),
        name="attn_sample",
    )(page_table.reshape(-1), q, k_new, v_new, *([cache_k] * pages), *([cache_v] * pages), *lams, g_subln)


def _gla_level_matrices(c):
    levels = c.bit_length() - 1
    t = lax.broadcasted_iota(jnp.int32, (c, c), 0)
    s = lax.broadcasted_iota(jnp.int32, (c, c), 1)
    tril = (s <= t).astype(F32)
    blocks, masks = [tril], []
    for l in range(1, levels + 1):
        grp, half = 1 << l, 1 << (l - 1)
        rho = (t // grp) * grp + half - 1
        blocks.append(tril - (s <= rho).astype(F32))
        masks.append(((t // grp) == (s // grp)) & ((t % grp) >= half) & ((s % grp) < half))
    return jnp.concatenate(blocks, axis=0), masks, (s == t)


def _split3_dot(m3, g):
    g1 = g.astype(BF16)
    r1 = g - g1.astype(F32)
    g2 = r1.astype(BF16)
    g3 = (r1 - g2.astype(F32)).astype(BF16)
    return _dot(m3, jnp.concatenate([g1, g2, g3], axis=0))


def _gla_kernel(qk_ref, v_ref, la_ref, gr_ref, g_ref, s0_ref, o_ref, sfin_ref, s_ref, *, chunk, n_chunks):
    @pl.when(pl.program_id(1) == 0)
    def _():
        s_ref[...] = s0_ref[...]

    mstack, masks, eye = _gla_level_matrices(chunk)
    m3 = jnp.concatenate([mstack.astype(BF16)] * 3, axis=1)
    eye_dk = (lax.broadcasted_iota(jnp.int32, (GLA_DK, GLA_DK), 0)
              == lax.broadcasted_iota(jnp.int32, (GLA_DK, GLA_DK), 1))

    def do_chunk(ci, carry):
        r0 = pl.multiple_of(ci * chunk, chunk)
        rows = pl.ds(r0, chunk)
        ex_all = _split3_dot(m3, la_ref[rows, :] * LOG2_E)
        for h in range(GLA_HEADS):
            kcols = slice(h * GLA_DK, (h + 1) * GLA_DK)
            vcols = slice(h * GLA_DV, (h + 1) * GLA_DV)
            q = qk_ref[rows, kcols]
            k = qk_ref[rows, GLA_QK + h * GLA_DK:GLA_QK + (h + 1) * GLA_DK]
            v = v_ref[rows, vcols].astype(BF16)
            state = s_ref[h]

            ex = ex_all[:, kcols]
            b = ex[:chunk]
            blast = b[chunk - 1:chunk]
            o = _dot((q * jnp.exp2(b)).astype(BF16), state.astype(BF16))
            att = jnp.where(eye, _dot_nt(q.astype(BF16), k.astype(BF16)), 0.0)
            for l, mask in enumerate(masks):
                d = ex[(l + 1) * chunk:(l + 2) * chunk]
                a_l = (q * jnp.exp2(jnp.minimum(d, 0.0))).astype(BF16)
                b_l = (k * jnp.exp2(jnp.minimum(-d, 0.0))).astype(BF16)
                att = jnp.where(mask, _dot_nt(a_l, b_l), att)
            o = o + _dot(att.astype(BF16), v)

            kdec = (k * jnp.exp2(blast - b)).astype(BF16)
            dec_col = jnp.sum(jnp.where(eye_dk, jnp.exp2(blast), 0.0), axis=1, keepdims=True)
            s_ref[h] = dec_col * state + _dot_tn(kdec, v)

            gate = _silu(gr_ref[rows, vcols])
            o_ref[rows, vcols] = (_rms(o, g_ref[...]) * gate).astype(o_ref.dtype)
        return carry

    lax.fori_loop(0, n_chunks, do_chunk, 0)

    @pl.when(pl.program_id(1) == pl.num_programs(1) - 1)
    def _():
        sfin_ref[...] = s_ref[...]


def _gla(qk, v, la, gr, g_gla, s0, nb, out_dtype):
    n = qk.shape[0]
    t = n // nb
    chunk = min(GLA_CHUNK, t)
    assert t % chunk == 0 and chunk & (chunk - 1) == 0
    tt = _tile(t, 512)
    n_t = t // tt
    row = lambda b, i: (b * n_t + i, 0)
    st = lambda b, i: (b, 0, 0, 0)
    return pl.pallas_call(
        functools.partial(_gla_kernel, chunk=chunk, n_chunks=tt // chunk),
        grid=(nb, n_t),
        in_specs=[pl.BlockSpec((tt, 2 * GLA_QK), row), pl.BlockSpec((tt, GLA_V), row),
                  pl.BlockSpec((tt, GLA_QK), row), pl.BlockSpec((tt, GLA_V), row),
                  pl.BlockSpec((1, GLA_DV), lambda b, i: (0, 0)),
                  pl.BlockSpec((None, GLA_HEADS, GLA_DK, GLA_DV), st)],
        out_specs=[pl.BlockSpec((tt, GLA_V), row),
                   pl.BlockSpec((None, GLA_HEADS, GLA_DK, GLA_DV), st)],
        out_shape=[jax.ShapeDtypeStruct((n, GLA_V), out_dtype),
                   jax.ShapeDtypeStruct((nb, GLA_HEADS, GLA_DK, GLA_DV), F32)],
        scratch_shapes=[pltpu.VMEM((GLA_HEADS, GLA_DK, GLA_DV), F32)],
        compiler_params=_cparams(("parallel", "arbitrary")),
        name="gla",
    )(qk, v, la, gr, g_gla, s0)


def _outproj_kernel(od_ref, og_ref, x_ref, w_ref, g_ref, h_ref, hn_ref, wb_ref):
    @pl.when(pl.program_id(0) == 0)
    def _():
        wb_ref[...] = w_ref[...].astype(BF16)

    h = (x_ref[...] + _dot(od_ref[...].astype(BF16), wb_ref[:DIFF_V, :])
         + _dot(og_ref[...].astype(BF16), wb_ref[DIFF_V:, :]))
    h_ref[...] = h
    hn_ref[...] = _rms(h, g_ref[...]).astype(hn_ref.dtype)


def _outproj(od, og, x, w_out, g_ffn):
    n = x.shape[0]
    tm = _tile(n, 512)
    return pl.pallas_call(
        _outproj_kernel,
        grid=(n // tm,),
        in_specs=[pl.BlockSpec((tm, DIFF_V), lambda i: (i, 0)), pl.BlockSpec((tm, GLA_V), lambda i: (i, 0)),
                  pl.BlockSpec((tm, D_MODEL), lambda i: (i, 0)),
                  pl.BlockSpec((D_MODEL, D_MODEL), lambda i: (0, 0), pipeline_mode=pl.Buffered(1)),
                  pl.BlockSpec((1, D_MODEL), lambda i: (0, 0))],
        out_specs=[pl.BlockSpec((tm, D_MODEL), lambda i: (i, 0)), pl.BlockSpec((tm, D_MODEL), lambda i: (i, 0))],
        out_shape=[jax.ShapeDtypeStruct((n, D_MODEL), F32), jax.ShapeDtypeStruct((n, D_MODEL), BF16)],
        scratch_shapes=[pltpu.VMEM((D_MODEL, D_MODEL), BF16)],
        compiler_params=_cparams(("arbitrary",)),
        name="outproj",
    )(od, og, x, w_out, g_ffn)


def _ffn_kernel(hn_ref, h_ref, wg_ref, wv_ref, wc_ref, bc_ref, wd_ref, gf_ref, p1_ref, p2_ref,
                y_ref, gate_ref, carry_ref, *, seq, tf, rb):
    i, j = pl.program_id(0), pl.program_id(1)
    tm = hn_ref.shape[0]

    @pl.when(j == 0)
    def _():
        y_ref[...] = h_ref[...]

    if not seq:
        cols = pl.ds(pl.multiple_of(j * tf, tf), tf)

        @pl.when(i == 0)
        def _():
            carry_ref[:, cols] = p1_ref[...]
        prev = carry_ref[:, cols]
    wc = wc_ref[...]
    for r in range(tm // rb):
        rows = slice(r * rb, (r + 1) * rb)
        hn = hn_ref[rows, :]
        gate = _dot(hn, wg_ref[...])
        val = _dot(hn, wv_ref[...])
        row = lax.broadcasted_iota(jnp.int32, gate.shape, 0)
        g1 = pltpu.roll(gate, 1, 0)
        g2 = pltpu.roll(gate, 2, 0)
        if seq:
            t = row % seq
            g1 = jnp.where(t == 0, p1_ref[rows, :], g1)
            g2 = jnp.where(t < 2, p2_ref[rows, :], g2)
            gate_ref[rows, :] = gate
        else:
            last, last2 = prev[SUBLANES - 1:SUBLANES], prev[SUBLANES - 2:SUBLANES - 1]
            g1 = jnp.where(row == 0, last, g1)
            g2 = jnp.where(row == 0, last2, jnp.where(row == 1, last, g2))
            prev = gate[rb - SUBLANES:]
        conv = bc_ref[...] + wc[0:1] * g2 + wc[1:2] * g1 + wc[2:3] * gate
        u = (_silu(conv) * val).astype(BF16)
        y_ref[rows, :] += _dot(u, wd_ref[...])
    if not seq:
        carry_ref[:, cols] = prev
        gate_ref[...] = prev

    @pl.when(j == pl.num_programs(1) - 1)
    def _():
        y_ref[...] = _rms(y_ref[...], gf_ref[...])


def _ffn(hn, h, w_up, w_conv, b_conv, w_down, g_final, p1, p2, seq):
    n = hn.shape[0]
    tm = _tile(n, 1024)
    tf = 512
    assert D_FF % tf == 0
    n_j = D_FF // tf
    if seq:
        assert n == tm and tm % seq == 0
        prev_spec = pl.BlockSpec((tm, tf), lambda i, j: (0, j))
        gate_rows = tm
    else:
        prev_spec = pl.BlockSpec((SUBLANES, tf), lambda i, j: (0, j))
        gate_rows = SUBLANES
    return pl.pallas_call(
        functools.partial(_ffn_kernel, seq=seq, tf=tf, rb=_tile(tm, 256)),
        grid=(n // tm, n_j),
        in_specs=[pl.BlockSpec((tm, D_MODEL), lambda i, j: (i, 0)),
                  pl.BlockSpec((tm, D_MODEL), lambda i, j: (i, 0)),
                  pl.BlockSpec((D_MODEL, tf), lambda i, j: (0, j)),
                  pl.BlockSpec((D_MODEL, tf), lambda i, j: (0, n_j + j)),
                  pl.BlockSpec((CONV_W, tf), lambda i, j: (0, j)),
                  pl.BlockSpec((1, tf), lambda i, j: (0, j)),
                  pl.BlockSpec((tf, D_MODEL), lambda i, j: (j, 0)),
                  pl.BlockSpec((1, D_MODEL), lambda i, j: (0, 0)),
                  prev_spec, prev_spec],
        out_specs=[pl.BlockSpec((tm, D_MODEL), lambda i, j: (i, 0)),
                   pl.BlockSpec((gate_rows, tf), lambda i, j: (i, j))],
        out_shape=[jax.ShapeDtypeStruct((n, D_MODEL), F32),
                   jax.ShapeDtypeStruct((n // tm * gate_rows, D_FF), F32)],
        scratch_shapes=[pltpu.VMEM((SUBLANES, D_FF), F32)],
        compiler_params=_cparams(("arbitrary", "arbitrary")),
        name="ffn",
    )(hn, h, w_up, w_up, w_conv, b_conv, w_down, g_final, p1, p2)


def _mixer_inputs(x, pos, wts, sample):
    xn, cos, sin = _norm(x, wts["g_mix"], pos, wts["freq_lane"])
    w = wts["w_in_t"]
    wide = DIFF_QK
    assert DIFF_V == wide and 2 * GLA_QK == wide and GLA_V == wide and wide % GLA_RANK == 0
    act = F32 if sample else BF16
    kv_std = () if sample else (BF16,)
    dq, *dk = _proj(xn, w, 0, (("rope_q", (act,), False), ("rope_k", kv_std, True)), (cos, sin))
    *dv, gqk = _proj(xn, w, 2 * wide, (("plain", kv_std, True), ("gla_qk", (F32,), False)))
    gv, gr = _proj(xn, w, 4 * wide, (("plain", (act,), False), ("plain", (F32,), False)))
    (la,) = _proj(xn, w, 6 * wide, (("gate", (F32,), False),), (wts["w_g2"], wts["b_g"]))
    return dq, dk, dv, gqk, gv, gr, la


def kernel(x_prompt, x_sample, cache_k, cache_v, state_gla, state_ffn_conv, page_table, g_mix, w_in, lam_q1, lam_k1, lam_q2, lam_k2, g_subln, w_g2, b_g, g_gla, w_out, g_ffn, w_up, w_conv, b_conv, w_down, g_final):
    nbp, t, _ = x_prompt.shape
    nbs, td, _ = x_sample.shape
    assert nbp == 1 and g_mix.shape[0] == 1
    n_phys, page = cache_k.shape[1], cache_k.shape[2]
    past_len = page_table.shape[1] * page

    inv_freq = ROPE_THETA ** (-jnp.arange(ROT_HALF, dtype=F32) * (2.0 / ROT_DIM))
    freq_lane = jnp.concatenate([inv_freq, inv_freq, jnp.zeros((LANES - ROT_DIM,), F32)]).reshape(1, LANES)
    row2 = lambda a: a[0].reshape(1, -1)
    wts = dict(g_mix=row2(g_mix), freq_lane=freq_lane, w_in_t=w_in[0].T, w_g2=w_g2[0], b_g=row2(b_g))
    lams = tuple(row2(a) for a in (lam_q1, lam_k1, lam_q2, lam_k2))
    g_sub, g_gl, g_ff, g_fin = row2(g_subln), row2(g_gla), row2(g_ffn), g_final.reshape(1, D_MODEL)
    w_out0, w_up_b, w_down_b = w_out[0], w_up[0].astype(BF16), w_down[0].astype(BF16)
    w_conv0, b_conv0 = w_conv[0], row2(b_conv)

    xp = x_prompt.reshape(t, D_MODEL)
    pos_p = jnp.arange(t, dtype=jnp.int32).astype(F32).reshape(t, 1)
    dq, (dk_b, dk), (dv_b, dv), gqk, gv, gr, la = _mixer_inputs(xp, pos_p, wts, sample=False)
    od = _attn_prompt(dq, dk_b, dv_b, lams, g_sub)
    og, s_p = _gla(gqk, gv, la, gr, g_gl, jnp.zeros((1, GLA_HEADS, GLA_DK, GLA_DV), F32), 1, BF16)
    h, hn = _outproj(od, og, xp, w_out0, g_ff)
    buf0 = jnp.zeros((SUBLANES, D_FF), F32)
    y_p, tail = _ffn(hn, h, w_up_b, w_conv0, b_conv0, w_down_b, g_fin, buf0, buf0, seq=0)
    conv_p = tail[tail.shape[0] - (CONV_W - 1):]

    ns = nbs * td
    xs = x_sample.reshape(ns, D_MODEL)
    pos_s = jnp.tile(past_len + jnp.arange(td, dtype=jnp.int32), nbs).astype(F32).reshape(ns, 1)
    dq_s, (dk_s,), (dv_s,), gqk_s, gv_s, gr_s, la_s = _mixer_inputs(xs, pos_s, wts, sample=True)
    ck = cache_k.reshape(n_phys * page * 2 * DIFF_HEADS, DIFF_DK)
    cv = (cache_v.reshape(n_phys, page, DIFF_HEADS, DIFF_DV // LANES, LANES)
          .transpose(0, 1, 3, 2, 4).reshape(n_phys * page * 2 * DIFF_HEADS, LANES))
    od_s = _attn_sample(dq_s, dk_s, dv_s, ck, cv, page_table, lams, g_sub, td, page)
    og_s, s_s = _gla(gqk_s, gv_s, la_s, gr_s, g_gl, state_gla[0], nbs, F32)
    h_s, hn_s = _outproj(od_s, og_s, xs, w_out0, g_ff)
    buf = state_ffn_conv[0]
    zeros = jnp.zeros((nbs, td, D_FF), F32)
    p1 = zeros.at[:, 0].set(buf[:, 1]).reshape(ns, D_FF)
    p2 = zeros.at[:, 0].set(buf[:, 0]).at[:, 1].set(buf[:, 1]).reshape(ns, D_FF)
    y_s, gate_s = _ffn(hn_s, h_s, w_up_b, w_conv0, b_conv0, w_down_b, g_fin, p1, p2, seq=td)
    conv_s = jnp.concatenate([buf, gate_s.reshape(nbs, td, D_FF)], axis=1)[:, td:]

    def v_out(rows, lead):
        a = rows.reshape(-1, DIFF_DV // LANES, DIFF_HEADS, LANES).transpose(0, 2, 1, 3)
        return a.reshape(*lead, DIFF_HEADS, DIFF_DV)

    return (y_p.reshape(1, t, D_MODEL), y_s.reshape(nbs, td, D_MODEL),
            dk.reshape(1, 1, t, DIFF_HEADS, 2, DIFF_DK), v_out(dv, (1, 1, t)),
            s_p.reshape(1, 1, GLA_HEADS, GLA_DK, GLA_DV), conv_p.reshape(1, 1, CONV_W - 1, D_FF),
            dk_s.reshape(1, nbs, td, DIFF_HEADS, 2, DIFF_DK), v_out(dv_s, (1, nbs, td)),
            s_s.reshape(1, nbs, GLA_HEADS, GLA_DK, GLA_DV), conv_s.reshape(1, nbs, CONV_W - 1, D_FF))
```

```python
import functools
import math

import jax
import jax.numpy as jnp
from jax import lax
from jax.experimental import pallas as pl
from jax.experimental.pallas import tpu as pltpu

F32 = jnp.float32
BF16 = jnp.bfloat16

D_MODEL = 2048
DIFF_HEADS = 4
DIFF_DK = 128
DIFF_DV = 256
GLA_HEADS = 4
GLA_DK = 128
GLA_DV = 256
GLA_RANK = 16
GLA_TAU = 16.0
GLA_CHUNK = 128
ROPE_THETA = 500000.0
ROT_DIM = DIFF_DK // 4
ROT_HALF = ROT_DIM // 2
D_FF = 5632
CONV_W = 3
EPS = 1e-6
LAM_INIT = 0.8 - 0.6 * math.exp(-0.3 * 0)
LOG2_E = math.log2(math.e)

DIFF_QK = DIFF_HEADS * 2 * DIFF_DK
DIFF_V = DIFF_HEADS * DIFF_DV
GLA_QK = GLA_HEADS * GLA_DK
GLA_V = GLA_HEADS * GLA_DV
LANES = 128
SUBLANES = 8

VMEM_LIMIT = 56 * 1024 * 1024

PAGES_PER_STEP = 16
NEG_INF = float("-inf")


def _cparams(sem):
    return pltpu.CompilerParams(dimension_semantics=sem, vmem_limit_bytes=VMEM_LIMIT)


def _tile(n, pref):
    t = min(n, pref)
    assert n % t == 0, (n, pref)
    return t


def _rms(x, g):
    return x * lax.rsqrt(jnp.mean(x * x, axis=-1, keepdims=True) + EPS) * g


def _silu(x):
    return x * (1.0 / (1.0 + jnp.exp(-x)))


def _dot(a, b):
    return jnp.dot(a, b, preferred_element_type=F32)


def _dot_nt(a, b):
    return lax.dot_general(a, b, (((1,), (1,)), ((), ())), preferred_element_type=F32)


def _dot_tn(a, b):
    return lax.dot_general(a, b, (((0,), (0,)), ((), ())), preferred_element_type=F32)


def _norm_kernel(x_ref, g_ref, pos_ref, freq_ref, xn_ref, cos_ref, sin_ref):
    xn_ref[...] = _rms(x_ref[...], g_ref[...]).astype(xn_ref.dtype)
    ang = pos_ref[...] * freq_ref[...]
    lane = lax.broadcasted_iota(jnp.int32, ang.shape, 1)
    s = jnp.sin(ang)
    cos_ref[...] = jnp.cos(ang)
    sin_ref[...] = jnp.where(lane < ROT_HALF, -s, s)


def _norm(x, g, pos, freq_lane):
    n = x.shape[0]
    tm = _tile(n, 512)
    return pl.pallas_call(
        _norm_kernel,
        grid=(n // tm,),
        in_specs=[pl.BlockSpec((tm, D_MODEL), lambda i: (i, 0)),
                  pl.BlockSpec((1, D_MODEL), lambda i: (0, 0)),
                  pl.BlockSpec((tm, 1), lambda i: (i, 0)),
                  pl.BlockSpec((1, LANES), lambda i: (0, 0))],
        out_specs=[pl.BlockSpec((tm, D_MODEL), lambda i: (i, 0)),
                   pl.BlockSpec((tm, LANES), lambda i: (i, 0)),
                   pl.BlockSpec((tm, LANES), lambda i: (i, 0))],
        out_shape=[jax.ShapeDtypeStruct((n, D_MODEL), BF16),
                   jax.ShapeDtypeStruct((n, LANES), F32),
                   jax.ShapeDtypeStruct((n, LANES), F32)],
        compiler_params=_cparams(("parallel",)),
        name="norm",
    )(x, g, pos, freq_lane)


def _rope(y, cos, sin):
    lane = lax.broadcasted_iota(jnp.int32, cos.shape, 1)
    outs = []
    for s in range(y.shape[1] // LANES):
        x = y[:, s * LANES:(s + 1) * LANES]
        partner = jnp.where(lane < ROT_HALF,
                            pltpu.roll(x, LANES - ROT_HALF, 1),
                            pltpu.roll(x, ROT_HALF, 1))
        outs.append(x * cos + partner * sin)
    return jnp.concatenate(outs, axis=1)


def _store_cache_rows(o_ref, y, r0, kind):
    rb = y.shape[0]
    n_rows = DIFF_QK // LANES
    for s in range(n_rows):
        if kind == "rope_k":
            dst = s
        else:
            h, half = divmod(s, DIFF_DV // LANES)
            dst = half * DIFF_HEADS + h
        o_ref[pl.ds(r0 * n_rows + dst, rb, stride=n_rows), :] = y[:, s * LANES:(s + 1) * LANES]


def _proj_kernel(*refs, parts, n_extra, gate, rb):
    wb_ref = refs[-1]
    xn_ref, w_ref = refs[:2]
    extras = refs[2:2 + n_extra]
    n_in = 2 + n_extra + (3 if gate else 0)
    outs = refs[n_in:-1]
    pw = w_ref.shape[0] // len(parts)

    if gate:
        wlr_ref, w2_ref, b_ref = refs[2 + n_extra:n_in]

    @pl.when(pl.program_id(0) == 0)
    def _():
        for p in range(len(parts)):
            wb_ref[:, p * pw:(p + 1) * pw] = w_ref[p * pw:(p + 1) * pw, :].T.astype(BF16)
        if gate:
            wl = jnp.concatenate([wlr_ref[...], jnp.zeros((LANES - GLA_RANK, D_MODEL), F32)], axis=0)
            wb_ref[:, len(parts) * pw:] = wl.T.astype(BF16)

    for r0 in range(0, xn_ref.shape[0], rb):
        rows = slice(r0, r0 + rb)
        y_all = _dot(xn_ref[rows, :], wb_ref[...])
        o = 0
        for p, (kind, n_std, cache_rows) in enumerate(parts):
            y = y_all[:, p * pw:(p + 1) * pw]
            if kind == "rope_q":
                y = _rope(y, extras[0][rows, :], extras[1][rows, :]) * (DIFF_DK ** -0.5 * LOG2_E)
            elif kind == "rope_k":
                y = _rope(y, extras[0][rows, :], extras[1][rows, :])
            elif kind == "gla_qk":
                col = lax.broadcasted_iota(jnp.int32, y.shape, 1)
                y = jnp.where(col < GLA_QK, y * (GLA_DK ** -0.5), y)
            for oref in outs[o:o + n_std]:
                oref[rows, :] = y.astype(oref.dtype)
            o += n_std
            if cache_rows:
                _store_cache_rows(outs[o], y, r0, kind)
                o += 1
        if gate:
            glr = y_all[:, len(parts) * pw:len(parts) * pw + GLA_RANK]
            z = _dot(glr.astype(BF16), w2_ref[...].astype(BF16)) + b_ref[...]
            outs[o][rows, :] = (jnp.minimum(z, 0.0) - jnp.log1p(jnp.exp(-jnp.abs(z)))) / GLA_TAU


def _proj(xn, wt, first_row, parts, extra=(), gate=None):
    n = xn.shape[0]
    tm = _tile(n, 1024)
    width = DIFF_QK
    total = width * len(parts)
    assert first_row % total == 0
    operands = [xn, wt] + list(extra)
    in_specs = [pl.BlockSpec((tm, D_MODEL), lambda i: (i, 0)),
                pl.BlockSpec((total, D_MODEL), lambda i: (first_row // total, 0), pipeline_mode=pl.Buffered(1))]
    for e in extra:
        in_specs.append(pl.BlockSpec((tm, e.shape[1]), lambda i: (i, 0)))
    out_specs, out_shape, kparts = [], [], []
    for kind, out_dtypes, cache_rows in parts:
        out_specs += [pl.BlockSpec((tm, width), lambda i: (i, 0)) for _ in out_dtypes]
        out_shape += [jax.ShapeDtypeStruct((n, width), dt) for dt in out_dtypes]
        if cache_rows:
            per_tok = width // LANES
            out_specs.append(pl.BlockSpec((tm * per_tok, LANES), lambda i: (i, 0)))
            out_shape.append(jax.ShapeDtypeStruct((n * per_tok, LANES), F32))
        kparts.append((kind, len(out_dtypes), cache_rows))
    if gate is not None:
        gate_row, w_g2, b_g = gate
        assert gate_row % GLA_RANK == 0
        operands += [wt, w_g2, b_g]
        in_specs += [pl.BlockSpec((GLA_RANK, D_MODEL), lambda i: (gate_row // GLA_RANK, 0)),
                     pl.BlockSpec(w_g2.shape, lambda i: (0, 0)), pl.BlockSpec(b_g.shape, lambda i: (0, 0))]
        out_specs.append(pl.BlockSpec((tm, GLA_QK), lambda i: (i, 0)))
        out_shape.append(jax.ShapeDtypeStruct((n, GLA_QK), F32))
    return pl.pallas_call(
        functools.partial(_proj_kernel, parts=tuple(kparts), n_extra=len(extra), gate=gate is not None,
                          rb=_tile(tm, 256)),
        grid=(n // tm,),
        in_specs=in_specs,
        out_specs=out_specs,
        out_shape=out_shape,
        scratch_shapes=[pltpu.VMEM((D_MODEL, total + (LANES if gate is not None else 0)), BF16)],
        compiler_params=_cparams(("arbitrary",)),
        name="proj_" + "_".join(k for k, _, _ in parts),
    )(*operands)


def _lam(lq1, lk1, lq2, lk2):
    return (jnp.exp(jnp.sum(lq1 * lk1, axis=-1, keepdims=True))
            - jnp.exp(jnp.sum(lq2 * lk2, axis=-1, keepdims=True)) + LAM_INIT)


def _attn_prompt_kernel(q_ref, k_ref, v_ref, lq1, lk1, lq2, lk2, g_ref, o_ref,
                        m_ref, l_ref, acc_ref, *, tq, rb):
    qi = pl.program_id(1)
    m_ref[...] = jnp.full(m_ref.shape, NEG_INF, F32)
    l_ref[...] = jnp.zeros(l_ref.shape, F32)
    acc_ref[...] = jnp.zeros(acc_ref.shape, F32)

    def block(kb, diagonal):
        r0 = pl.multiple_of(kb * tq, tq)
        k = k_ref[pl.ds(r0, tq), :]
        v = v_ref[pl.ds(r0, tq), :]
        for r in range(2 * tq // rb):
            amap, q0 = divmod(r * rb, tq)
            rows = slice(r * rb, (r + 1) * rb)
            dcols = slice(amap * DIFF_DK, (amap + 1) * DIFF_DK)
            nk = tq
            s = _dot_nt(q_ref[q0:q0 + rb, dcols], k[:nk, dcols])
            if diagonal:
                row = q0 + lax.broadcasted_iota(jnp.int32, s.shape, 0)
                col = lax.broadcasted_iota(jnp.int32, s.shape, 1)
                s = jnp.where(col <= row, s, NEG_INF)
            chunks = [s[:, c0:c0 + LANES] for c0 in range(0, nk, LANES)]
            mx = functools.reduce(jnp.maximum, chunks)
            m_old = m_ref[rows]
            m_new = jnp.maximum(m_old, jnp.max(mx, axis=-1, keepdims=True))
            alpha = jnp.exp2(m_old - m_new)
            ps = [jnp.exp2(c - m_new) for c in chunks]
            l_ref[rows] = alpha * l_ref[rows] + functools.reduce(jnp.add, ps)
            pv = _dot(jnp.concatenate(ps, axis=1).astype(BF16), v[:nk])
            acc_ref[rows] = jnp.concatenate([alpha] * (DIFF_DV // LANES), axis=1) * acc_ref[rows] + pv
            m_ref[rows] = m_new

    def body(kb, carry):
        block(kb, False)
        return carry

    lax.fori_loop(0, qi, body, 0)
    block(qi, True)

    o = acc_ref[...] / jnp.sum(l_ref[...], axis=-1, keepdims=True)
    lam = _lam(lq1[...], lk1[...], lq2[...], lk2[...])
    o = o[:tq] - lam * o[tq:]
    o_ref[...] = (_rms(o, g_ref[...]) * (1.0 - LAM_INIT)).astype(o_ref.dtype)


def _attn_prompt(q, k, v, lams, g_subln):
    t = q.shape[0]
    tq = _tile(t, 1024)
    rb = _tile(tq, 128)
    vec = pl.BlockSpec((1, DIFF_DK), lambda h, i: (0, 0))
    return pl.pallas_call(
        functools.partial(_attn_prompt_kernel, tq=tq, rb=rb),
        grid=(DIFF_HEADS, t // tq),
        in_specs=[pl.BlockSpec((tq, 2 * DIFF_DK), lambda h, i: (i, h)),
                  pl.BlockSpec((t, 2 * DIFF_DK), lambda h, i: (0, h)),
                  pl.BlockSpec((t, DIFF_DV), lambda h, i: (0, h)),
                  vec, vec, vec, vec,
                  pl.BlockSpec((1, DIFF_DV), lambda h, i: (0, 0))],
        out_specs=pl.BlockSpec((tq, DIFF_DV), lambda h, i: (i, h)),
        out_shape=jax.ShapeDtypeStruct((t, DIFF_V), BF16),
        scratch_shapes=[pltpu.VMEM((2 * tq, LANES), F32), pltpu.VMEM((2 * tq, LANES), F32),
                        pltpu.VMEM((2 * tq, DIFF_DV), F32)],
        compiler_params=_cparams(("parallel", "parallel")),
        name="attn_prompt",
    )(q, k, v, *lams, g_subln)


def _attn_sample_kernel(pt_ref, q_ref, kn_ref, vn_ref, *rest, td, pages, page):
    del pt_ref
    k_pages = rest[:pages]
    v_pages = rest[pages:2 * pages]
    lq1, lk1, lq2, lk2, g_ref, o_ref, m_ref, l_ref, acc_ref = rest[2 * pages:]
    c = pl.program_id(1)
    n_hm = 2 * DIFF_HEADS
    halves = DIFF_DV // LANES

    @pl.when(c == 0)
    def _():
        m_ref[...] = jnp.full(m_ref.shape, NEG_INF, F32)
        l_ref[...] = jnp.zeros(l_ref.shape, F32)
        acc_ref[...] = jnp.zeros(acc_ref.shape, F32)

    q = q_ref[...].astype(BF16)

    def q_hm(hm):
        return q[:, hm * DIFF_DK:(hm + 1) * DIFF_DK]

    def k_rows(refs, n_tok, hm):
        return jnp.concatenate([r[pl.ds(hm, n_tok, stride=n_hm), :].astype(BF16) for r in refs], axis=0)

    def v_rows(refs, n_tok, h):
        return jnp.concatenate(
            [jnp.concatenate([r[pl.ds(half * DIFF_HEADS + h, n_tok, stride=halves * DIFF_HEADS), :].astype(BF16)
                              for half in range(halves)], axis=1) for r in refs], axis=0)

    def update(s, v_of_head):
        m_old = m_ref[...]
        m_new = jnp.maximum(m_old, jnp.max(s, axis=-1, keepdims=True))
        alpha = jnp.exp2(m_old - m_new)
        p = jnp.exp2(s - m_new)
        l_ref[...] = alpha * l_ref[...] + jnp.sum(p, axis=-1, keepdims=True)
        pb = p.astype(BF16)
        pv = jnp.concatenate([_dot(pb[2 * h * td:2 * (h + 1) * td], v_of_head(h))
                              for h in range(DIFF_HEADS)], axis=0)
        acc_ref[...] = alpha * acc_ref[...] + pv
        m_ref[...] = m_new

    update(jnp.concatenate([_dot_nt(q_hm(hm), k_rows(k_pages, page, hm)) for hm in range(n_hm)], axis=0),
           functools.partial(v_rows, v_pages, page))

    @pl.when(c == pl.num_programs(1) - 1)
    def _():
        s = jnp.concatenate([_dot_nt(q_hm(hm), k_rows([kn_ref], td, hm)) for hm in range(n_hm)], axis=0)
        t_q = lax.broadcasted_iota(jnp.int32, s.shape, 0) % td
        t_k = lax.broadcasted_iota(jnp.int32, s.shape, 1)
        update(jnp.where(t_k <= t_q, s, NEG_INF), functools.partial(v_rows, [vn_ref], td))
        o = acc_ref[...] / l_ref[...]
        lam = _lam(lq1[...], lk1[...], lq2[...], lk2[...])
        for h in range(DIFF_HEADS):
            r1 = 2 * h * td
            oh = o[r1:r1 + td] - lam * o[r1 + td:r1 + 2 * td]
            o_ref[:, h * DIFF_DV:(h + 1) * DIFF_DV] = (_rms(oh, g_ref[...]) * (1.0 - LAM_INIT)).astype(o_ref.dtype)


def _attn_sample(q, k_new, v_new, cache_k, cache_v, page_table, lams, g_subln, td, page):
    nb, n_pages = page_table.shape
    pages = PAGES_PER_STEP
    assert n_pages % pages == 0
    rows = 2 * DIFF_HEADS * td
    page_rows = page * DIFF_QK // LANES

    def tok(b, c, pt):
        return (b, 0)

    def page_spec(i):
        return pl.BlockSpec((page_rows, LANES), lambda b, c, pt: (pt[b * n_pages + c * pages + i], 0))

    const = lambda shape: pl.BlockSpec(shape, lambda b, c, pt: (0, 0))
    grid_spec = pltpu.PrefetchScalarGridSpec(
        num_scalar_prefetch=1,
        grid=(nb, n_pages // pages),
        in_specs=[pl.BlockSpec((td, DIFF_QK), tok), pl.BlockSpec((td * DIFF_QK // LANES, LANES), tok),
                  pl.BlockSpec((td * DIFF_V // LANES, LANES), tok)]
                 + [page_spec(i) for i in range(pages)] + [page_spec(i) for i in range(pages)]
                 + [const((1, DIFF_DK))] * 4 + [const((1, DIFF_DV))],
        out_specs=pl.BlockSpec((td, DIFF_V), tok),
        scratch_shapes=[pltpu.VMEM((rows, 1), F32), pltpu.VMEM((rows, 1), F32),
                        pltpu.VMEM((rows, DIFF_DV), F32)],
    )
    return pl.pallas_call(
        functools.partial(_attn_sample_kernel, td=td, pages=pages, page=page),
        grid_spec=grid_spec,
        out_shape=jax.ShapeDtypeStruct((nb * td, DIFF_V), F32),
        compiler_params=_cparams(("parallel", "arbitrary")),
        name="attn_sample",
    )(page_table.reshape(-1), q, k_new, v_new, *([cache_k] * pages), *([cache_v] * pages), *lams, g_subln)


def _gla_level_matrices(c):
    levels = c.bit_length() - 1
    t = lax.broadcasted_iota(jnp.int32, (c, c), 0)
    s = lax.broadcasted_iota(jnp.int32, (c, c), 1)
    tril = (s <= t).astype(F32)
    blocks, masks = [tril], []
    for l in range(1, levels + 1):
        grp, half = 1 << l, 1 << (l - 1)
        rho = (t // grp) * grp + half - 1
        blocks.append(tril - (s <= rho).astype(F32))
        masks.append(((t // grp) == (s // grp)) & ((t % grp) >= half) & ((s % grp) < half))
    return jnp.concatenate(blocks, axis=0), masks, (s == t)


def _split3_dot(m3, g):
    g1 = g.astype(BF16)
    r1 = g - g1.astype(F32)
    g2 = r1.astype(BF16)
    g3 = (r1 - g2.astype(F32)).astype(BF16)
    return _dot(m3, jnp.concatenate([g1, g2, g3], axis=0))


def _gla_kernel(qk_ref, v_ref, la_ref, gr_ref, g_ref, s0_ref, o_ref, sfin_ref, s_ref, *, chunk, n_chunks):
    @pl.when(pl.program_id(1) == 0)
    def _():
        s_ref[...] = s0_ref[...]

    mstack, masks, eye = _gla_level_matrices(chunk)
    m3 = jnp.concatenate([mstack.astype(BF16)] * 3, axis=1)
    eye_dk = (lax.broadcasted_iota(jnp.int32, (GLA_DK, GLA_DK), 0)
              == lax.broadcasted_iota(jnp.int32, (GLA_DK, GLA_DK), 1))

    def do_chunk(ci, carry):
        r0 = pl.multiple_of(ci * chunk, chunk)
        rows = pl.ds(r0, chunk)
        ex_all = _split3_dot(m3, la_ref[rows, :] * LOG2_E)
        for h in range(GLA_HEADS):
            kcols = slice(h * GLA_DK, (h + 1) * GLA_DK)
            vcols = slice(h * GLA_DV, (h + 1) * GLA_DV)
            q = qk_ref[rows, kcols]
            k = qk_ref[rows, GLA_QK + h * GLA_DK:GLA_QK + (h + 1) * GLA_DK]
            v = v_ref[rows, vcols].astype(BF16)
            state = s_ref[h]

            ex = ex_all[:, kcols]
            b = ex[:chunk]
            blast = b[chunk - 1:chunk]
            o = _dot((q * jnp.exp2(b)).astype(BF16), state.astype(BF16))
            att = jnp.where(eye, _dot_nt(q.astype(BF16), k.astype(BF16)), 0.0)
            for l, mask in enumerate(masks):
                d = ex[(l + 1) * chunk:(l + 2) * chunk]
                a_l = (q * jnp.exp2(jnp.minimum(d, 0.0))).astype(BF16)
                b_l = (k * jnp.exp2(jnp.minimum(-d, 0.0))).astype(BF16)
                att = jnp.where(mask, _dot_nt(a_l, b_l), att)
            o = o + _dot(att.astype(BF16), v)

            kdec = (k * jnp.exp2(blast - b)).astype(BF16)
            dec_col = jnp.sum(jnp.where(eye_dk, jnp.exp2(blast), 0.0), axis=1, keepdims=True)
            s_ref[h] = dec_col * state + _dot_tn(kdec, v)

            gate = _silu(gr_ref[rows, vcols])
            o_ref[rows, vcols] = (_rms(o, g_ref[...]) * gate).astype(o_ref.dtype)
        return carry

    lax.fori_loop(0, n_chunks, do_chunk, 0)

    @pl.when(pl.program_id(1) == pl.num_programs(1) - 1)
    def _():
        sfin_ref[...] = s_ref[...]


def _gla(qk, v, la, gr, g_gla, s0, nb, out_dtype):
    n = qk.shape[0]
    t = n // nb
    chunk = min(GLA_CHUNK, t)
    assert t % chunk == 0 and chunk & (chunk - 1) == 0
    tt = _tile(t, 512)
    n_t = t // tt
    row = lambda b, i: (b * n_t + i, 0)
    st = lambda b, i: (b, 0, 0, 0)
    return pl.pallas_call(
        functools.partial(_gla_kernel, chunk=chunk, n_chunks=tt // chunk),
        grid=(nb, n_t),
        in_specs=[pl.BlockSpec((tt, 2 * GLA_QK), row), pl.BlockSpec((tt, GLA_V), row),
                  pl.BlockSpec((tt, GLA_QK), row), pl.BlockSpec((tt, GLA_V), row),
                  pl.BlockSpec((1, GLA_DV), lambda b, i: (0, 0)),
                  pl.BlockSpec((None, GLA_HEADS, GLA_DK, GLA_DV), st)],
        out_specs=[pl.BlockSpec((tt, GLA_V), row),
                   pl.BlockSpec((None, GLA_HEADS, GLA_DK, GLA_DV), st)],
        out_shape=[jax.ShapeDtypeStruct((n, GLA_V), out_dtype),
                   jax.ShapeDtypeStruct((nb, GLA_HEADS, GLA_DK, GLA_DV), F32)],
        scratch_shapes=[pltpu.VMEM((GLA_HEADS, GLA_DK, GLA_DV), F32)],
        compiler_params=_cparams(("parallel", "arbitrary")),
        name="gla",
    )(qk, v, la, gr, g_gla, s0)


def _outproj_kernel(od_ref, og_ref, x_ref, w_ref, g_ref, h_ref, hn_ref, wb_ref):
    @pl.when(pl.program_id(0) == 0)
    def _():
        wb_ref[...] = w_ref[...].astype(BF16)

    h = (x_ref[...] + _dot(od_ref[...].astype(BF16), wb_ref[:DIFF_V, :])
         + _dot(og_ref[...].astype(BF16), wb_ref[DIFF_V:, :]))
    h_ref[...] = h
    hn_ref[...] = _rms(h, g_ref[...]).astype(hn_ref.dtype)


def _outproj(od, og, x, w_out, g_ffn):
    n = x.shape[0]
    tm = _tile(n, 512)
    return pl.pallas_call(
        _outproj_kernel,
        grid=(n // tm,),
        in_specs=[pl.BlockSpec((tm, DIFF_V), lambda i: (i, 0)), pl.BlockSpec((tm, GLA_V), lambda i: (i, 0)),
                  pl.BlockSpec((tm, D_MODEL), lambda i: (i, 0)),
                  pl.BlockSpec((D_MODEL, D_MODEL), lambda i: (0, 0), pipeline_mode=pl.Buffered(1)),
                  pl.BlockSpec((1, D_MODEL), lambda i: (0, 0))],
        out_specs=[pl.BlockSpec((tm, D_MODEL), lambda i: (i, 0)), pl.BlockSpec((tm, D_MODEL), lambda i: (i, 0))],
        out_shape=[jax.ShapeDtypeStruct((n, D_MODEL), F32), jax.ShapeDtypeStruct((n, D_MODEL), BF16)],
        scratch_shapes=[pltpu.VMEM((D_MODEL, D_MODEL), BF16)],
        compiler_params=_cparams(("arbitrary",)),
        name="outproj",
    )(od, og, x, w_out, g_ffn)


def _ffn_kernel(hn_ref, h_ref, wg_ref, wv_ref, wc_ref, bc_ref, wd_ref, gf_ref, p1_ref, p2_ref,
                y_ref, gate_ref, carry_ref, *, seq, tf, rb):
    i, j = pl.program_id(0), pl.program_id(1)
    tm = hn_ref.shape[0]

    @pl.when(j == 0)
    def _():
        y_ref[...] = h_ref[...]

    if not seq:
        cols = pl.ds(pl.multiple_of(j * tf, tf), tf)

        @pl.when(i == 0)
        def _():
            carry_ref[:, cols] = p1_ref[...]
        prev = carry_ref[:, cols]
    wc = wc_ref[...]
    for r in range(tm // rb):
        rows = slice(r * rb, (r + 1) * rb)
        hn = hn_ref[rows, :]
        gate = _dot(hn, wg_ref[...])
        val = _dot(hn, wv_ref[...])
        row = lax.broadcasted_iota(jnp.int32, gate.shape, 0)
        g1 = pltpu.roll(gate, 1, 0)
        g2 = pltpu.roll(gate, 2, 0)
        if seq:
            t = row % seq
            g1 = jnp.where(t == 0, p1_ref[rows, :], g1)
            g2 = jnp.where(t < 2, p2_ref[rows, :], g2)
            gate_ref[rows, :] = gate
        else:
            last, last2 = prev[SUBLANES - 1:SUBLANES], prev[SUBLANES - 2:SUBLANES - 1]
            g1 = jnp.where(row == 0, last, g1)
            g2 = jnp.where(row == 0, last2, jnp.where(row == 1, last, g2))
            prev = gate[rb - SUBLANES:]
        conv = bc_ref[...] + wc[0:1] * g2 + wc[1:2] * g1 + wc[2:3] * gate
        u = (_silu(conv) * val).astype(BF16)
        y_ref[rows, :] += _dot(u, wd_ref[...])
    if not seq:
        carry_ref[:, cols] = prev
        gate_ref[...] = prev

    @pl.when(j == pl.num_programs(1) - 1)
    def _():
        y_ref[...] = _rms(y_ref[...], gf_ref[...])


def _ffn(hn, h, w_up, w_conv, b_conv, w_down, g_final, p1, p2, seq):
    n = hn.shape[0]
    tm = _tile(n, 1024)
    tf = 512
    assert D_FF % tf == 0
    n_j = D_FF // tf
    if seq:
        assert n == tm and tm % seq == 0
        prev_spec = pl.BlockSpec((tm, tf), lambda i, j: (0, j))
        gate_rows = tm
    else:
        prev_spec = pl.BlockSpec((SUBLANES, tf), lambda i, j: (0, j))
        gate_rows = SUBLANES
    return pl.pallas_call(
        functools.partial(_ffn_kernel, seq=seq, tf=tf, rb=_tile(tm, 256)),
        grid=(n // tm, n_j),
        in_specs=[pl.BlockSpec((tm, D_MODEL), lambda i, j: (i, 0)),
                  pl.BlockSpec((tm, D_MODEL), lambda i, j: (i, 0)),
                  pl.BlockSpec((D_MODEL, tf), lambda i, j: (0, j)),
                  pl.BlockSpec((D_MODEL, tf), lambda i, j: (0, n_j + j)),
                  pl.BlockSpec((CONV_W, tf), lambda i, j: (0, j)),
                  pl.BlockSpec((1, tf), lambda i, j: (0, j)),
                  pl.BlockSpec((tf, D_MODEL), lambda i, j: (j, 0)),
                  pl.BlockSpec((1, D_MODEL), lambda i, j: (0, 0)),
                  prev_spec, prev_spec],
        out_specs=[pl.BlockSpec((tm, D_MODEL), lambda i, j: (i, 0)),
                   pl.BlockSpec((gate_rows, tf), lambda i, j: (i, j))],
        out_shape=[jax.ShapeDtypeStruct((n, D_MODEL), F32),
                   jax.ShapeDtypeStruct((n // tm * gate_rows, D_FF), F32)],
        scratch_shapes=[pltpu.VMEM((SUBLANES, D_FF), F32)],
        compiler_params=_cparams(("arbitrary", "arbitrary")),
        name="ffn",
    )(hn, h, w_up, w_up, w_conv, b_conv, w_down, g_final, p1, p2)


def _mixer_inputs(x, pos, wts, sample):
    xn, cos, sin = _norm(x, wts["g_mix"], pos, wts["freq_lane"])
    w = wts["w_in_t"]
    wide = DIFF_QK
    assert DIFF_V == wide and 2 * GLA_QK == wide and GLA_V == wide and wide % GLA_RANK == 0
    act = F32 if sample else BF16
    kv_std = () if sample else (BF16,)
    dq, *dk = _proj(xn, w, 0, (("rope_q", (act,), False), ("rope_k", kv_std, True)), (cos, sin))
    *dv, gqk = _proj(xn, w, 2 * wide, (("plain", kv_std, True), ("gla_qk", (F32,), False)))
    gv, gr, la = _proj(xn, w, 4 * wide, (("plain", (act,), False), ("plain", (F32,), False)),
                       gate=(6 * wide, wts["w_g2"], wts["b_g"]))
    return dq, dk, dv, gqk, gv, gr, la


def kernel(x_prompt, x_sample, cache_k, cache_v, state_gla, state_ffn_conv, page_table, g_mix, w_in, lam_q1, lam_k1, lam_q2, lam_k2, g_subln, w_g2, b_g, g_gla, w_out, g_ffn, w_up, w_conv, b_conv, w_down, g_final):
    nbp, t, _ = x_prompt.shape
    nbs, td, _ = x_sample.shape
    assert nbp == 1 and g_mix.shape[0] == 1
    n_phys, page = cache_k.shape[1], cache_k.shape[2]
    past_len = page_table.shape[1] * page

    inv_freq = ROPE_THETA ** (-jnp.arange(ROT_HALF, dtype=F32) * (2.0 / ROT_DIM))
    freq_lane = jnp.concatenate([inv_freq, inv_freq, jnp.zeros((LANES - ROT_DIM,), F32)]).reshape(1, LANES)
    row2 = lambda a: a[0].reshape(1, -1)
    wts = dict(g_mix=row2(g_mix), freq_lane=freq_lane, w_in_t=w_in[0].T, w_g2=w_g2[0], b_g=row2(b_g))
    lams = tuple(row2(a) for a in (lam_q1, lam_k1, lam_q2, lam_k2))
    g_sub, g_gl, g_ff, g_fin = row2(g_subln), row2(g_gla), row2(g_ffn), g_final.reshape(1, D_MODEL)
    w_out0, w_up_b, w_down_b = w_out[0], w_up[0].astype(BF16), w_down[0].astype(BF16)
    w_conv0, b_conv0 = w_conv[0], row2(b_conv)

    xp = x_prompt.reshape(t, D_MODEL)
    pos_p = jnp.arange(t, dtype=jnp.int32).astype(F32).reshape(t, 1)
    dq, (dk_b, dk), (dv_b, dv), gqk, gv, gr, la = _mixer_inputs(xp, pos_p, wts, sample=False)
    od = _attn_prompt(dq, dk_b, dv_b, lams, g_sub)
    og, s_p = _gla(gqk, gv, la, gr, g_gl, jnp.zeros((1, GLA_HEADS, GLA_DK, GLA_DV), F32), 1, BF16)
    h, hn = _outproj(od, og, xp, w_out0, g_ff)
    buf0 = jnp.zeros((SUBLANES, D_FF), F32)
    y_p, tail = _ffn(hn, h, w_up_b, w_conv0, b_conv0, w_down_b, g_fin, buf0, buf0, seq=0)
    conv_p = tail[tail.shape[0] - (CONV_W - 1):]

    ns = nbs * td
    xs = x_sample.reshape(ns, D_MODEL)
    pos_s = jnp.tile(past_len + jnp.arange(td, dtype=jnp.int32), nbs).astype(F32).reshape(ns, 1)
    dq_s, (dk_s,), (dv_s,), gqk_s, gv_s, gr_s, la_s = _mixer_inputs(xs, pos_s, wts, sample=True)
    ck = cache_k.reshape(n_phys * page * 2 * DIFF_HEADS, DIFF_DK)
    cv = (cache_v.reshape(n_phys, page, DIFF_HEADS, DIFF_DV // LANES, LANES)
          .transpose(0, 1, 3, 2, 4).reshape(n_phys * page * 2 * DIFF_HEADS, LANES))
    od_s = _attn_sample(dq_s, dk_s, dv_s, ck, cv, page_table, lams, g_sub, td, page)
    og_s, s_s = _gla(gqk_s, gv_s, la_s, gr_s, g_gl, state_gla[0], nbs, F32)
    h_s, hn_s = _outproj(od_s, og_s, xs, w_out0, g_ff)
    buf = state_ffn_conv[0]
    tpos = jnp.arange(td).reshape(1, td, 1)
    p1 = jnp.where(tpos == 0, buf[:, 1:2], 0.0).reshape(ns, D_FF)
    p2 = jnp.where(tpos == 0, buf[:, 0:1], jnp.where(tpos == 1, buf[:, 1:2], 0.0)).reshape(ns, D_FF)
    y_s, gate_s = _ffn(hn_s, h_s, w_up_b, w_conv0, b_conv0, w_down_b, g_fin, p1, p2, seq=td)
    conv_s = jnp.concatenate([buf, gate_s.reshape(nbs, td, D_FF)], axis=1)[:, td:]

    def v_out(rows, lead):
        a = rows.reshape(-1, DIFF_DV // LANES, DIFF_HEADS, LANES).transpose(0, 2, 1, 3)
        return a.reshape(*lead, DIFF_HEADS, DIFF_DV)

    return (y_p.reshape(1, t, D_MODEL), y_s.reshape(nbs, td, D_MODEL),
            dk.reshape(1, 1, t, DIFF_HEADS, 2, DIFF_DK), v_out(dv, (1, 1, t)),
            s_p.reshape(1, 1, GLA_HEADS, GLA_DK, GLA_DV), conv_p.reshape(1, 1, CONV_W - 1, D_FF),
            dk_s.reshape(1, nbs, td, DIFF_HEADS, 2, DIFF_DK), v_out(dv_s, (1, nbs, td)),
            s_s.reshape(1, nbs, GLA_HEADS, GLA_DK, GLA_DV), conv_s.reshape(1, nbs, CONV_W - 1, D_FF))
```

```python
import functools
import math

import jax
import jax.numpy as jnp
from jax import lax
from jax.experimental import pallas as pl
from jax.experimental.pallas import tpu as pltpu

F32 = jnp.float32
BF16 = jnp.bfloat16

D_MODEL = 2048
DIFF_HEADS = 4
DIFF_DK = 128
DIFF_DV = 256
GLA_HEADS = 4
GLA_DK = 128
GLA_DV = 256
GLA_RANK = 16
GLA_TAU = 16.0
GLA_CHUNK = 128
ROPE_THETA = 500000.0
ROT_DIM = DIFF_DK // 4
ROT_HALF = ROT_DIM // 2
POS_PER_ROW = 128 // ROT_DIM
D_FF = 5632
CONV_W = 3
EPS = 1e-6
LAM_INIT = 0.8 - 0.6 * math.exp(-0.3 * 0)
LOG2_E = math.log2(math.e)

DIFF_QK = DIFF_HEADS * 2 * DIFF_DK
DIFF_V = DIFF_HEADS * DIFF_DV
GLA_QK = GLA_HEADS * GLA_DK
GLA_V = GLA_HEADS * GLA_DV
LANES = 128
SUBLANES = 8

VMEM_LIMIT = 56 * 1024 * 1024

PAGES_PER_STEP = 16
GLA_SEQS_PER_STEP = 4
NEG_INF = float("-inf")


def _cparams(sem):
    return pltpu.CompilerParams(dimension_semantics=sem, vmem_limit_bytes=VMEM_LIMIT)


def _tile(n, pref):
    t = min(n, pref)
    assert n % t == 0, (n, pref)
    return t


def _rms(x, g):
    return x * lax.rsqrt(jnp.mean(x * x, axis=-1, keepdims=True) + EPS) * g


def _silu(x):
    return x * (1.0 / (1.0 + jnp.exp(-x)))


def _dot(a, b):
    return jnp.dot(a, b, preferred_element_type=F32)


def _dot_nt(a, b):
    return lax.dot_general(a, b, (((1,), (1,)), ((), ())), preferred_element_type=F32)


def _dot_tn(a, b):
    return lax.dot_general(a, b, (((0,), (0,)), ((), ())), preferred_element_type=F32)


def _norm_kernel(x_ref, g_ref, pos_ref, freq_ref, xn_ref, cos_ref, sin_ref):
    xn_ref[...] = _rms(x_ref[...], g_ref[...]).astype(xn_ref.dtype)
    ang = pos_ref[...] * freq_ref[...]
    lane = lax.broadcasted_iota(jnp.int32, ang.shape, 1)
    c = jnp.cos(ang)
    s = jnp.sin(ang)
    s = jnp.where(lane % ROT_DIM < ROT_HALF, -s, s)
    n_rows = ang.shape[0]
    for p in range(POS_PER_ROW):
        shift = (LANES - p * ROT_DIM) % LANES
        cp = c if shift == 0 else pltpu.roll(c, shift, 1)
        sp = s if shift == 0 else pltpu.roll(s, shift, 1)
        cos_ref[pl.ds(p, n_rows, stride=POS_PER_ROW), :] = jnp.where(lane < ROT_DIM, cp, 1.0)
        sin_ref[pl.ds(p, n_rows, stride=POS_PER_ROW), :] = jnp.where(lane < ROT_DIM, sp, 0.0)


def _norm(x, g, pos, freq_lane):
    n = x.shape[0]
    tm = _tile(n, 1024)
    return pl.pallas_call(
        _norm_kernel,
        grid=(n // tm,),
        in_specs=[pl.BlockSpec((tm, D_MODEL), lambda i: (i, 0)),
                  pl.BlockSpec((1, D_MODEL), lambda i: (0, 0)),
                  pl.BlockSpec((tm // POS_PER_ROW, LANES), lambda i: (i, 0)),
                  pl.BlockSpec((1, LANES), lambda i: (0, 0))],
        out_specs=[pl.BlockSpec((tm, D_MODEL), lambda i: (i, 0)),
                   pl.BlockSpec((tm, LANES), lambda i: (i, 0)),
                   pl.BlockSpec((tm, LANES), lambda i: (i, 0))],
        out_shape=[jax.ShapeDtypeStruct((n, D_MODEL), BF16),
                   jax.ShapeDtypeStruct((n, LANES), F32),
                   jax.ShapeDtypeStruct((n, LANES), F32)],
        compiler_params=_cparams(("parallel",)),
        name="norm",
    )(x, g, pos, freq_lane)


def _rope(y, cos, sin):
    lane = lax.broadcasted_iota(jnp.int32, cos.shape, 1)
    outs = []
    for s in range(y.shape[1] // LANES):
        x = y[:, s * LANES:(s + 1) * LANES]
        partner = jnp.where(lane < ROT_HALF,
                            pltpu.roll(x, LANES - ROT_HALF, 1),
                            pltpu.roll(x, ROT_HALF, 1))
        outs.append(x * cos + partner * sin)
    return jnp.concatenate(outs, axis=1)


def _store_cache_rows(o_ref, y, r0, kind):
    rb = y.shape[0]
    n_rows = DIFF_QK // LANES
    for s in range(n_rows):
        if kind == "rope_k":
            dst = s
        else:
            h, half = divmod(s, DIFF_DV // LANES)
            dst = half * DIFF_HEADS + h
        o_ref[pl.ds(r0 * n_rows + dst, rb, stride=n_rows), :] = y[:, s * LANES:(s + 1) * LANES]


def _proj_kernel(*refs, parts, n_extra, gate, rb):
    wb_ref = refs[-1]
    xn_ref, w_ref = refs[:2]
    extras = refs[2:2 + n_extra]
    n_in = 2 + n_extra + (3 if gate else 0)
    outs = refs[n_in:-1]
    pw = w_ref.shape[0] // len(parts)

    if gate:
        wlr_ref, w2_ref, b_ref = refs[2 + n_extra:n_in]

    @pl.when(pl.program_id(0) == 0)
    def _():
        for p in range(len(parts)):
            wb_ref[:, p * pw:(p + 1) * pw] = w_ref[p * pw:(p + 1) * pw, :].T.astype(BF16)
        if gate:
            wl = jnp.concatenate([wlr_ref[...], jnp.zeros((LANES - GLA_RANK, D_MODEL), F32)], axis=0)
            wb_ref[:, len(parts) * pw:] = wl.T.astype(BF16)

    for r0 in range(0, xn_ref.shape[0], rb):
        rows = slice(r0, r0 + rb)
        y_all = _dot(xn_ref[rows, :], wb_ref[...])
        o = 0
        for p, (kind, n_std, cache_rows) in enumerate(parts):
            y = y_all[:, p * pw:(p + 1) * pw]
            if kind == "rope_q":
                y = _rope(y, extras[0][rows, :], extras[1][rows, :]) * (DIFF_DK ** -0.5 * LOG2_E)
            elif kind == "rope_k":
                y = _rope(y, extras[0][rows, :], extras[1][rows, :])
            elif kind == "gla_qk":
                col = lax.broadcasted_iota(jnp.int32, y.shape, 1)
                y = jnp.where(col < GLA_QK, y * (GLA_DK ** -0.5), y)
            for oref in outs[o:o + n_std]:
                oref[rows, :] = y.astype(oref.dtype)
            o += n_std
            if cache_rows:
                _store_cache_rows(outs[o], y, r0, kind)
                o += 1
        if gate:
            glr = y_all[:, len(parts) * pw:len(parts) * pw + GLA_RANK]
            z = _dot(glr.astype(BF16), w2_ref[...].astype(BF16)) + b_ref[...]
            outs[o][rows, :] = (jnp.minimum(z, 0.0) - jnp.log1p(jnp.exp(-jnp.abs(z)))) / GLA_TAU


def _proj(xn, wt, first_row, parts, extra=(), gate=None):
    n = xn.shape[0]
    tm = _tile(n, 1024)
    width = DIFF_QK
    total = width * len(parts)
    assert first_row % total == 0
    operands = [xn, wt] + list(extra)
    in_specs = [pl.BlockSpec((tm, D_MODEL), lambda i: (i, 0)),
                pl.BlockSpec((total, D_MODEL), lambda i: (first_row // total, 0), pipeline_mode=pl.Buffered(1))]
    for e in extra:
        in_specs.append(pl.BlockSpec((tm, e.shape[1]), lambda i: (i, 0)))
    out_specs, out_shape, kparts = [], [], []
    for kind, out_dtypes, cache_rows in parts:
        out_specs += [pl.BlockSpec((tm, width), lambda i: (i, 0)) for _ in out_dtypes]
        out_shape += [jax.ShapeDtypeStruct((n, width), dt) for dt in out_dtypes]
        if cache_rows:
            per_tok = width // LANES
            out_specs.append(pl.BlockSpec((tm * per_tok, LANES), lambda i: (i, 0)))
            out_shape.append(jax.ShapeDtypeStruct((n * per_tok, LANES), F32))
        kparts.append((kind, len(out_dtypes), cache_rows))
    if gate is not None:
        gate_row, w_g2, b_g = gate
        assert gate_row % GLA_RANK == 0
        operands += [wt, w_g2, b_g]
        in_specs += [pl.BlockSpec((GLA_RANK, D_MODEL), lambda i: (gate_row // GLA_RANK, 0)),
                     pl.BlockSpec(w_g2.shape, lambda i: (0, 0)), pl.BlockSpec(b_g.shape, lambda i: (0, 0))]
        out_specs.append(pl.BlockSpec((tm, GLA_QK), lambda i: (i, 0)))
        out_shape.append(jax.ShapeDtypeStruct((n, GLA_QK), F32))
    return pl.pallas_call(
        functools.partial(_proj_kernel, parts=tuple(kparts), n_extra=len(extra), gate=gate is not None,
                          rb=_tile(tm, 256)),
        grid=(n // tm,),
        in_specs=in_specs,
        out_specs=out_specs,
        out_shape=out_shape,
        scratch_shapes=[pltpu.VMEM((D_MODEL, total + (LANES if gate is not None else 0)), BF16)],
        compiler_params=_cparams(("arbitrary",)),
        name="proj_" + "_".join(k for k, _, _ in parts),
    )(*operands)


def _lam(lq1, lk1, lq2, lk2):
    return (jnp.exp(jnp.sum(lq1 * lk1, axis=-1, keepdims=True))
            - jnp.exp(jnp.sum(lq2 * lk2, axis=-1, keepdims=True)) + LAM_INIT)


def _attn_prompt_kernel(q_ref, k_ref, v_ref, lq1, lk1, lq2, lk2, g_ref, o_ref,
                        m_ref, l_ref, acc_ref, *, tq, rb):
    qi = pl.program_id(1)
    m_ref[...] = jnp.full(m_ref.shape, NEG_INF, F32)
    l_ref[...] = jnp.zeros(l_ref.shape, F32)
    acc_ref[...] = jnp.zeros(acc_ref.shape, F32)

    def block(kb, diagonal):
        r0 = pl.multiple_of(kb * tq, tq)
        k = k_ref[pl.ds(r0, tq), :]
        v = v_ref[pl.ds(r0, tq), :]
        for r in range(2 * tq // rb):
            amap, q0 = divmod(r * rb, tq)
            rows = slice(r * rb, (r + 1) * rb)
            dcols = slice(amap * DIFF_DK, (amap + 1) * DIFF_DK)
            nk = tq
            s = _dot_nt(q_ref[q0:q0 + rb, dcols], k[:nk, dcols])
            if diagonal:
                row = q0 + lax.broadcasted_iota(jnp.int32, s.shape, 0)
                col = lax.broadcasted_iota(jnp.int32, s.shape, 1)
                s = jnp.where(col <= row, s, NEG_INF)
            chunks = [s[:, c0:c0 + LANES] for c0 in range(0, nk, LANES)]
            mx = functools.reduce(jnp.maximum, chunks)
            m_old = m_ref[rows]
            m_new = jnp.maximum(m_old, jnp.max(mx, axis=-1, keepdims=True))
            alpha = jnp.exp2(m_old - m_new)
            ps = [jnp.exp2(c - m_new) for c in chunks]
            l_ref[rows] = alpha * l_ref[rows] + functools.reduce(jnp.add, ps)
            pv = _dot(jnp.concatenate(ps, axis=1).astype(BF16), v[:nk])
            acc_ref[rows] = jnp.concatenate([alpha] * (DIFF_DV // LANES), axis=1) * acc_ref[rows] + pv
            m_ref[rows] = m_new

    def body(kb, carry):
        block(kb, False)
        return carry

    lax.fori_loop(0, qi, body, 0)
    block(qi, True)

    o = acc_ref[...] / jnp.sum(l_ref[...], axis=-1, keepdims=True)
    lam = _lam(lq1[...], lk1[...], lq2[...], lk2[...])
    o = o[:tq] - lam * o[tq:]
    o_ref[...] = (_rms(o, g_ref[...]) * (1.0 - LAM_INIT)).astype(o_ref.dtype)


def _attn_prompt(q, k, v, lams, g_subln):
    t = q.shape[0]
    tq = _tile(t, 1024)
    rb = _tile(tq, 128)
    vec = pl.BlockSpec((1, DIFF_DK), lambda h, i: (0, 0))
    return pl.pallas_call(
        functools.partial(_attn_prompt_kernel, tq=tq, rb=rb),
        grid=(DIFF_HEADS, t // tq),
        in_specs=[pl.BlockSpec((tq, 2 * DIFF_DK), lambda h, i: (i, h)),
                  pl.BlockSpec((t, 2 * DIFF_DK), lambda h, i: (0, h)),
                  pl.BlockSpec((t, DIFF_DV), lambda h, i: (0, h)),
                  vec, vec, vec, vec,
                  pl.BlockSpec((1, DIFF_DV), lambda h, i: (0, 0))],
        out_specs=pl.BlockSpec((tq, DIFF_DV), lambda h, i: (i, h)),
        out_shape=jax.ShapeDtypeStruct((t, DIFF_V), BF16),
        scratch_shapes=[pltpu.VMEM((2 * tq, LANES), F32), pltpu.VMEM((2 * tq, LANES), F32),
                        pltpu.VMEM((2 * tq, DIFF_DV), F32)],
        compiler_params=_cparams(("parallel", "parallel")),
        name="attn_prompt",
    )(q, k, v, *lams, g_subln)


def _attn_sample_kernel(pt_ref, q_ref, kn_ref, vn_ref, *rest, td, pages, page):
    del pt_ref
    k_pages = rest[:pages]
    v_pages = rest[pages:2 * pages]
    lq1, lk1, lq2, lk2, g_ref, o_ref, m_ref, l_ref, acc_ref = rest[2 * pages:]
    c = pl.program_id(1)
    n_hm = 2 * DIFF_HEADS
    halves = DIFF_DV // LANES

    @pl.when(c == 0)
    def _():
        m_ref[...] = jnp.full(m_ref.shape, NEG_INF, F32)
        l_ref[...] = jnp.zeros(l_ref.shape, F32)
        acc_ref[...] = jnp.zeros(acc_ref.shape, F32)

    q = q_ref[...].astype(BF16)

    def q_hm(hm):
        return q[:, hm * DIFF_DK:(hm + 1) * DIFF_DK]

    def k_rows(refs, n_tok, hm):
        return jnp.concatenate([r[pl.ds(hm, n_tok, stride=n_hm), :].astype(BF16) for r in refs], axis=0)

    def v_rows(refs, n_tok, h):
        return jnp.concatenate(
            [jnp.concatenate([r[pl.ds(half * DIFF_HEADS + h, n_tok, stride=halves * DIFF_HEADS), :].astype(BF16)
                              for half in range(halves)], axis=1) for r in refs], axis=0)

    def update(s, v_of_head):
        m_old = m_ref[...]
        m_new = jnp.maximum(m_old, jnp.max(s, axis=-1, keepdims=True))
        alpha = jnp.exp2(m_old - m_new)
        p = jnp.exp2(s - m_new)
        l_ref[...] = alpha * l_ref[...] + jnp.sum(p, axis=-1, keepdims=True)
        pb = p.astype(BF16)
        pv = jnp.concatenate([_dot(pb[2 * h * td:2 * (h + 1) * td], v_of_head(h))
                              for h in range(DIFF_HEADS)], axis=0)
        acc_ref[...] = alpha * acc_ref[...] + pv
        m_ref[...] = m_new

    update(jnp.concatenate([_dot_nt(q_hm(hm), k_rows(k_pages, page, hm)) for hm in range(n_hm)], axis=0),
           functools.partial(v_rows, v_pages, page))

    @pl.when(c == pl.num_programs(1) - 1)
    def _():
        s = jnp.concatenate([_dot_nt(q_hm(hm), k_rows([kn_ref], td, hm)) for hm in range(n_hm)], axis=0)
        t_q = lax.broadcasted_iota(jnp.int32, s.shape, 0) % td
        t_k = lax.broadcasted_iota(jnp.int32, s.shape, 1)
        update(jnp.where(t_k <= t_q, s, NEG_INF), functools.partial(v_rows, [vn_ref], td))
        o = acc_ref[...] / l_ref[...]
        lam = _lam(lq1[...], lk1[...], lq2[...], lk2[...])
        for h in range(DIFF_HEADS):
            r1 = 2 * h * td
            oh = o[r1:r1 + td] - lam * o[r1 + td:r1 + 2 * td]
            o_ref[:, h * DIFF_DV:(h + 1) * DIFF_DV] = (_rms(oh, g_ref[...]) * (1.0 - LAM_INIT)).astype(o_ref.dtype)


def _attn_sample(q, k_new, v_new, cache_k, cache_v, page_table, lams, g_subln, td, page):
    nb, n_pages = page_table.shape
    pages = PAGES_PER_STEP
    assert n_pages % pages == 0
    rows = 2 * DIFF_HEADS * td
    page_rows = page * DIFF_QK // LANES

    def tok(b, c, pt):
        return (b, 0)

    def page_spec(i):
        return pl.BlockSpec((page_rows, LANES), lambda b, c, pt: (pt[b * n_pages + c * pages + i], 0))

    const = lambda shape: pl.BlockSpec(shape, lambda b, c, pt: (0, 0))
    grid_spec = pltpu.PrefetchScalarGridSpec(
        num_scalar_prefetch=1,
        grid=(nb, n_pages // pages),
        in_specs=[pl.BlockSpec((td, DIFF_QK), tok), pl.BlockSpec((td * DIFF_QK // LANES, LANES), tok),
                  pl.BlockSpec((td * DIFF_V // LANES, LANES), tok)]
                 + [page_spec(i) for i in range(pages)] + [page_spec(i) for i in range(pages)]
                 + [const((1, DIFF_DK))] * 4 + [const((1, DIFF_DV))],
        out_specs=pl.BlockSpec((td, DIFF_V), tok),
        scratch_shapes=[pltpu.VMEM((rows, 1), F32), pltpu.VMEM((rows, 1), F32),
                        pltpu.VMEM((rows, DIFF_DV), F32)],
    )
    return pl.pallas_call(
        functools.partial(_attn_sample_kernel, td=td, pages=pages, page=page),
        grid_spec=grid_spec,
        out_shape=jax.ShapeDtypeStruct((nb * td, DIFF_V), F32),
        compiler_params=_cparams(("parallel", "arbitrary")),
        name="attn_sample",
    )(page_table.reshape(-1), q, k_new, v_new, *([cache_k] * pages), *([cache_v] * pages), *lams, g_subln)


def _gla_level_matrices(c):
    levels = c.bit_length() - 1
    t = lax.broadcasted_iota(jnp.int32, (c, c), 0)
    s = lax.broadcasted_iota(jnp.int32, (c, c), 1)
    tril = (s <= t).astype(F32)
    blocks, masks = [tril], []
    for l in range(1, levels + 1):
        grp, half = 1 << l, 1 << (l - 1)
        rho = (t // grp) * grp + half - 1
        blocks.append(tril - (s <= rho).astype(F32))
        masks.append(((t // grp) == (s // grp)) & ((t % grp) >= half) & ((s % grp) < half))
    return jnp.concatenate(blocks, axis=0), masks, (s == t)


def _split3_dot(m3, g):
    g1 = g.astype(BF16)
    r1 = g - g1.astype(F32)
    g2 = r1.astype(BF16)
    g3 = (r1 - g2.astype(F32)).astype(BF16)
    return _dot(m3, jnp.concatenate([g1, g2, g3], axis=0))


def _gla_kernel(qk_ref, v_ref, la_ref, gr_ref, g_ref, s0_ref, o_ref, sfin_ref, s_ref, *, chunk, n_chunks, bb):
    @pl.when(pl.program_id(1) == 0)
    def _():
        s_ref[...] = s0_ref[...]

    mstack, masks, eye = _gla_level_matrices(chunk)
    m3 = jnp.concatenate([mstack.astype(BF16)] * 3, axis=1)
    eye_dk = (lax.broadcasted_iota(jnp.int32, (GLA_DK, GLA_DK), 0)
              == lax.broadcasted_iota(jnp.int32, (GLA_DK, GLA_DK), 1))

    def do_chunk(ci, carry):
        for bi in range(bb):
            do_seq_chunk(bi, pl.multiple_of(bi * n_chunks * chunk + ci * chunk, chunk))
        return carry

    def do_seq_chunk(bi, r0):
        rows = pl.ds(r0, chunk)
        ex_all = _split3_dot(m3, la_ref[rows, :] * LOG2_E)
        for h in range(GLA_HEADS):
            kcols = slice(h * GLA_DK, (h + 1) * GLA_DK)
            vcols = slice(h * GLA_DV, (h + 1) * GLA_DV)
            q = qk_ref[rows, kcols]
            k = qk_ref[rows, GLA_QK + h * GLA_DK:GLA_QK + (h + 1) * GLA_DK]
            v = v_ref[rows, vcols].astype(BF16)
            state = s_ref[bi, h]

            ex = ex_all[:, kcols]
            b = ex[:chunk]
            blast = b[chunk - 1:chunk]
            o = _dot((q * jnp.exp2(b)).astype(BF16), state.astype(BF16))
            att = jnp.where(eye, _dot_nt(q.astype(BF16), k.astype(BF16)), 0.0)
            for l, mask in enumerate(masks):
                d = ex[(l + 1) * chunk:(l + 2) * chunk]
                a_l = (q * jnp.exp2(jnp.minimum(d, 0.0))).astype(BF16)
                b_l = (k * jnp.exp2(jnp.minimum(-d, 0.0))).astype(BF16)
                att = jnp.where(mask, _dot_nt(a_l, b_l), att)
            o = o + _dot(att.astype(BF16), v)

            kdec = (k * jnp.exp2(blast - b)).astype(BF16)
            dec_col = jnp.sum(jnp.where(eye_dk, jnp.exp2(blast), 0.0), axis=1, keepdims=True)
            s_ref[bi, h] = dec_col * state + _dot_tn(kdec, v)

            gate = _silu(gr_ref[rows, vcols])
            o_ref[rows, vcols] = (_rms(o, g_ref[...]) * gate).astype(o_ref.dtype)

    lax.fori_loop(0, n_chunks, do_chunk, 0)

    @pl.when(pl.program_id(1) == pl.num_programs(1) - 1)
    def _():
        sfin_ref[...] = s_ref[...]


def _gla(qk, v, la, gr, g_gla, s0, nb, out_dtype):
    n = qk.shape[0]
    t = n // nb
    chunk = min(GLA_CHUNK, t)
    assert t % chunk == 0 and chunk & (chunk - 1) == 0
    tt = _tile(t, 1024)
    n_t = t // tt
    bb = _tile(nb, GLA_SEQS_PER_STEP) if n_t == 1 else 1
    row = lambda b, i: (b * n_t + i, 0)
    st = lambda b, i: (b, 0, 0, 0)
    state_block = (bb, GLA_HEADS, GLA_DK, GLA_DV)
    return pl.pallas_call(
        functools.partial(_gla_kernel, chunk=chunk, n_chunks=tt // chunk, bb=bb),
        grid=(nb // bb, n_t),
        in_specs=[pl.BlockSpec((bb * tt, 2 * GLA_QK), row), pl.BlockSpec((bb * tt, GLA_V), row),
                  pl.BlockSpec((bb * tt, GLA_QK), row), pl.BlockSpec((bb * tt, GLA_V), row),
                  pl.BlockSpec((1, GLA_DV), lambda b, i: (0, 0)),
                  pl.BlockSpec(state_block, st)],
        out_specs=[pl.BlockSpec((bb * tt, GLA_V), row),
                   pl.BlockSpec(state_block, st)],
        out_shape=[jax.ShapeDtypeStruct((n, GLA_V), out_dtype),
                   jax.ShapeDtypeStruct((nb, GLA_HEADS, GLA_DK, GLA_DV), F32)],
        scratch_shapes=[pltpu.VMEM(state_block, F32)],
        compiler_params=_cparams(("parallel", "arbitrary")),
        name="gla",
    )(qk, v, la, gr, g_gla, s0)


def _outproj_kernel(od_ref, og_ref, x_ref, w_ref, g_ref, h_ref, hn_ref, wb_ref):
    @pl.when(pl.program_id(0) == 0)
    def _():
        wb_ref[...] = w_ref[...].astype(BF16)

    h = (x_ref[...] + _dot(od_ref[...].astype(BF16), wb_ref[:DIFF_V, :])
         + _dot(og_ref[...].astype(BF16), wb_ref[DIFF_V:, :]))
    h_ref[...] = h
    hn_ref[...] = _rms(h, g_ref[...]).astype(hn_ref.dtype)


def _outproj(od, og, x, w_out, g_ffn):
    n = x.shape[0]
    tm = _tile(n, 512)
    return pl.pallas_call(
        _outproj_kernel,
        grid=(n // tm,),
        in_specs=[pl.BlockSpec((tm, DIFF_V), lambda i: (i, 0)), pl.BlockSpec((tm, GLA_V), lambda i: (i, 0)),
                  pl.BlockSpec((tm, D_MODEL), lambda i: (i, 0)),
                  pl.BlockSpec((D_MODEL, D_MODEL), lambda i: (0, 0), pipeline_mode=pl.Buffered(1)),
                  pl.BlockSpec((1, D_MODEL), lambda i: (0, 0))],
        out_specs=[pl.BlockSpec((tm, D_MODEL), lambda i: (i, 0)), pl.BlockSpec((tm, D_MODEL), lambda i: (i, 0))],
        out_shape=[jax.ShapeDtypeStruct((n, D_MODEL), F32), jax.ShapeDtypeStruct((n, D_MODEL), BF16)],
        scratch_shapes=[pltpu.VMEM((D_MODEL, D_MODEL), BF16)],
        compiler_params=_cparams(("arbitrary",)),
        name="outproj",
    )(od, og, x, w_out, g_ffn)


def _ffn_kernel(hn_ref, h_ref, wg_ref, wv_ref, wc_ref, bc_ref, wd_ref, gf_ref, p1_ref, p2_ref,
                y_ref, gate_ref, carry_ref, *, seq, tf, rb):
    i, j = pl.program_id(0), pl.program_id(1)
    tm = hn_ref.shape[0]

    @pl.when(j == 0)
    def _():
        y_ref[...] = h_ref[...]

    if not seq:
        cols = pl.ds(pl.multiple_of(j * tf, tf), tf)

        @pl.when(i == 0)
        def _():
            carry_ref[:, cols] = p1_ref[...]
        prev = carry_ref[:, cols]
    wc = wc_ref[...]
    for r in range(tm // rb):
        rows = slice(r * rb, (r + 1) * rb)
        hn = hn_ref[rows, :]
        gate = _dot(hn, wg_ref[...])
        val = _dot(hn, wv_ref[...])
        row = lax.broadcasted_iota(jnp.int32, gate.shape, 0)
        g1 = pltpu.roll(gate, 1, 0)
        g2 = pltpu.roll(gate, 2, 0)
        if seq:
            t = row % seq
            g1 = jnp.where(t == 0, p1_ref[rows, :], g1)
            g2 = jnp.where(t < 2, p2_ref[rows, :], g2)
            gate_ref[rows, :] = gate
        else:
            last, last2 = prev[SUBLANES - 1:SUBLANES], prev[SUBLANES - 2:SUBLANES - 1]
            g1 = jnp.where(row == 0, last, g1)
            g2 = jnp.where(row == 0, last2, jnp.where(row == 1, last, g2))
            prev = gate[rb - SUBLANES:]
        conv = bc_ref[...] + wc[0:1] * g2 + wc[1:2] * g1 + wc[2:3] * gate
        u = (_silu(conv) * val).astype(BF16)
        y_ref[rows, :] += _dot(u, wd_ref[...])
    if not seq:
        carry_ref[:, cols] = prev
        gate_ref[...] = prev

    @pl.when(j == pl.num_programs(1) - 1)
    def _():
        y_ref[...] = _rms(y_ref[...], gf_ref[...])


def _ffn(hn, h, w_up, w_conv, b_conv, w_down, g_final, p1, p2, seq):
    n = hn.shape[0]
    tm = _tile(n, 1024)
    tf = 512
    assert D_FF % tf == 0
    n_j = D_FF // tf
    if seq:
        assert n == tm and tm % seq == 0
        prev_spec = pl.BlockSpec((tm, tf), lambda i, j: (0, j))
        gate_rows = tm
    else:
        prev_spec = pl.BlockSpec((SUBLANES, tf), lambda i, j: (0, j))
        gate_rows = SUBLANES
    return pl.pallas_call(
        functools.partial(_ffn_kernel, seq=seq, tf=tf, rb=_tile(tm, 256)),
        grid=(n // tm, n_j),
        in_specs=[pl.BlockSpec((tm, D_MODEL), lambda i, j: (i, 0)),
                  pl.BlockSpec((tm, D_MODEL), lambda i, j: (i, 0)),
                  pl.BlockSpec((D_MODEL, tf), lambda i, j: (0, j)),
                  pl.BlockSpec((D_MODEL, tf), lambda i, j: (0, n_j + j)),
                  pl.BlockSpec((CONV_W, tf), lambda i, j: (0, j)),
                  pl.BlockSpec((1, tf), lambda i, j: (0, j)),
                  pl.BlockSpec((tf, D_MODEL), lambda i, j: (j, 0)),
                  pl.BlockSpec((1, D_MODEL), lambda i, j: (0, 0)),
                  prev_spec, prev_spec],
        out_specs=[pl.BlockSpec((tm, D_MODEL), lambda i, j: (i, 0)),
                   pl.BlockSpec((gate_rows, tf), lambda i, j: (i, j))],
        out_shape=[jax.ShapeDtypeStruct((n, D_MODEL), F32),
                   jax.ShapeDtypeStruct((n // tm * gate_rows, D_FF), F32)],
        scratch_shapes=[pltpu.VMEM((SUBLANES, D_FF), F32)],
        compiler_params=_cparams(("arbitrary", "arbitrary")),
        name="ffn",
    )(hn, h, w_up, w_up, w_conv, b_conv, w_down, g_final, p1, p2)


def _mixer_inputs(x, pos, wts, sample):
    xn, cos, sin = _norm(x, wts["g_mix"], pos, wts["freq_lane"])
    w = wts["w_in_t"]
    wide = DIFF_QK
    assert DIFF_V == wide and 2 * GLA_QK == wide and GLA_V == wide and wide % GLA_RANK == 0
    act = F32 if sample else BF16
    kv_std = () if sample else (BF16,)
    dq, *dk = _proj(xn, w, 0, (("rope_q", (act,), False), ("rope_k", kv_std, True)), (cos, sin))
    *dv, gqk = _proj(xn, w, 2 * wide, (("plain", kv_std, True), ("gla_qk", (F32,), False)))
    gv, gr, la = _proj(xn, w, 4 * wide, (("plain", (act,), False), ("plain", (F32,), False)),
                       gate=(6 * wide, wts["w_g2"], wts["b_g"]))
    return dq, dk, dv, gqk, gv, gr, la


def kernel(x_prompt, x_sample, cache_k, cache_v, state_gla, state_ffn_conv, page_table, g_mix, w_in, lam_q1, lam_k1, lam_q2, lam_k2, g_subln, w_g2, b_g, g_gla, w_out, g_ffn, w_up, w_conv, b_conv, w_down, g_final):
    nbp, t, _ = x_prompt.shape
    nbs, td, _ = x_sample.shape
    assert nbp == 1 and g_mix.shape[0] == 1
    n_phys, page = cache_k.shape[1], cache_k.shape[2]
    past_len = page_table.shape[1] * page

    inv_freq = ROPE_THETA ** (-jnp.arange(ROT_HALF, dtype=F32) * (2.0 / ROT_DIM))
    freq_lane = jnp.tile(inv_freq, 2 * POS_PER_ROW).reshape(1, LANES)

    def pos_rows(pos):
        return jnp.repeat(pos.astype(F32).reshape(-1, POS_PER_ROW), ROT_DIM, axis=1)

    row2 = lambda a: a[0].reshape(1, -1)
    wts = dict(g_mix=row2(g_mix), freq_lane=freq_lane, w_in_t=w_in[0].T, w_g2=w_g2[0], b_g=row2(b_g))
    lams = tuple(row2(a) for a in (lam_q1, lam_k1, lam_q2, lam_k2))
    g_sub, g_gl, g_ff, g_fin = row2(g_subln), row2(g_gla), row2(g_ffn), g_final.reshape(1, D_MODEL)
    w_out0, w_up_b, w_down_b = w_out[0], w_up[0].astype(BF16), w_down[0].astype(BF16)
    w_conv0, b_conv0 = w_conv[0], row2(b_conv)

    xp = x_prompt.reshape(t, D_MODEL)
    pos_p = pos_rows(jnp.arange(t, dtype=jnp.int32))
    dq, (dk_b, dk), (dv_b, dv), gqk, gv, gr, la = _mixer_inputs(xp, pos_p, wts, sample=False)
    od = _attn_prompt(dq, dk_b, dv_b, lams, g_sub)
    og, s_p = _gla(gqk, gv, la, gr, g_gl, jnp.zeros((1, GLA_HEADS, GLA_DK, GLA_DV), F32), 1, BF16)
    h, hn = _outproj(od, og, xp, w_out0, g_ff)
    buf0 = jnp.zeros((SUBLANES, D_FF), F32)
    y_p, tail = _ffn(hn, h, w_up_b, w_conv0, b_conv0, w_down_b, g_fin, buf0, buf0, seq=0)
    conv_p = tail[tail.shape[0] - (CONV_W - 1):]

    ns = nbs * td
    xs = x_sample.reshape(ns, D_MODEL)
    pos_s = pos_rows(jnp.tile(past_len + jnp.arange(td, dtype=jnp.int32), nbs))
    dq_s, (dk_s,), (dv_s,), gqk_s, gv_s, gr_s, la_s = _mixer_inputs(xs, pos_s, wts, sample=True)
    ck = cache_k.reshape(n_phys * page * 2 * DIFF_HEADS, DIFF_DK)
    cv = (cache_v.reshape(n_phys, page, DIFF_HEADS, DIFF_DV // LANES, LANES)
          .transpose(0, 1, 3, 2, 4).reshape(n_phys * page * 2 * DIFF_HEADS, LANES))
    od_s = _attn_sample(dq_s, dk_s, dv_s, ck, cv, page_table, lams, g_sub, td, page)
    og_s, s_s = _gla(gqk_s, gv_s, la_s, gr_s, g_gl, state_gla[0], nbs, F32)
    h_s, hn_s = _outproj(od_s, og_s, xs, w_out0, g_ff)
    buf = state_ffn_conv[0]
    tpos = jnp.arange(td).reshape(1, td, 1)
    p1 = jnp.where(tpos == 0, buf[:, 1:2], 0.0).reshape(ns, D_FF)
    p2 = jnp.where(tpos == 0, buf[:, 0:1], jnp.where(tpos == 1, buf[:, 1:2], 0.0)).reshape(ns, D_FF)
    y_s, gate_s = _ffn(hn_s, h_s, w_up_b, w_conv0, b_conv0, w_down_b, g_fin, p1, p2, seq=td)
    conv_s = jnp.concatenate([buf, gate_s.reshape(nbs, td, D_FF)], axis=1)[:, td:]

    def v_out(rows, lead):
        a = rows.reshape(-1, DIFF_DV // LANES, DIFF_HEADS, LANES).transpose(0, 2, 1, 3)
        return a.reshape(*lead, DIFF_HEADS, DIFF_DV)

    return (y_p.reshape(1, t, D_MODEL), y_s.reshape(nbs, td, D_MODEL),
            dk.reshape(1, 1, t, DIFF_HEADS, 2, DIFF_DK), v_out(dv, (1, 1, t)),
            s_p.reshape(1, 1, GLA_HEADS, GLA_DK, GLA_DV), conv_p.reshape(1, 1, CONV_W - 1, D_FF),
            dk_s.reshape(1, nbs, td, DIFF_HEADS, 2, DIFF_DK), v_out(dv_s, (1, nbs, td)),
            s_s.reshape(1, nbs, GLA_HEADS, GLA_DK, GLA_DV), conv_s.reshape(1, nbs, CONV_W - 1, D_FF))
```
